```python
import math
import jax, jax.numpy as jnp
from jax import lax
import numpy as np

D_MODEL = 1024
BATCH = 4
SEQ = 8192
DEPTH = 4

FFN_DIM = ((8 * D_MODEL // 3 + 127) // 128) * 128
DN_ALPHA = (2.0 * DEPTH) ** 0.25
DN_BETA = (8.0 * DEPTH) ** -0.25
LN_EPS = 1e-5
CONV_DIM = D_MODEL // 2
CONV_WIDTH = 31
NSA_HEAD_DIM = 64
NSA_HEADS = (D_MODEL // 2) // NSA_HEAD_DIM
NSA_KV_GROUPS = max(1, NSA_HEADS // 4)
NSA_HPG = NSA_HEADS // NSA_KV_GROUPS
CMP_LEN = 32
CMP_STRIDE = 16
CMP_HIDDEN = 128
SEL_BLOCK = 64
SEL_TOPN = 16
WINDOW = 512
Q_BLOCK = 128
FORCE_SCORE = 1e4
NEG_INF = -1e30
EVEN_IN_COLS = 2 * CONV_DIM + NSA_HEADS * NSA_HEAD_DIM + 6 * NSA_KV_GROUPS * NSA_HEAD_DIM + 3 * NSA_HEADS
S5_GROUP = 16
S5_GROUPS = D_MODEL // S5_GROUP
S5_STATE = 64
S5_CHUNK = 128
DT_MIN = 0.001
DT_MAX = 0.1
N_EVEN = (DEPTH + 1) // 2
N_ODD = DEPTH // 2

kernel_name = 'hybrid_conformer_nsa_s5_deepnorm'


def layer_norm(x, g, b):
    xf = x.astype(jnp.float32)
    mu = jnp.mean(xf, axis=-1, keepdims=True)
    var = jnp.mean(jnp.square(xf - mu), axis=-1, keepdims=True)
    return ((xf - mu) * lax.rsqrt(var + LN_EPS) * g + b).astype(x.dtype)


def swiglu(x, w_in, w_out):
    gate, val = jnp.split(x @ w_in, 2, axis=-1)
    return (jax.nn.silu(gate) * val) @ w_out


def masked_softmax(s, mask):
    p = jax.nn.softmax(jnp.where(mask, s, NEG_INF), axis=-1)
    return jnp.where(mask, p, 0.0)


def alibi_slopes():
    h = jnp.arange(1, NSA_HEADS + 1, dtype=jnp.float32)
    return (2.0 ** (-8.0 * h / NSA_HEADS)).reshape(NSA_KV_GROUPS, NSA_HPG)


def causal_depthwise_conv(a, w, bias):
    out = lax.conv_general_dilated(a, w[:, None, :].astype(a.dtype), window_strides=(1,),
                                   padding=[(CONV_WIDTH - 1, 0)],
                                   dimension_numbers=('NWC', 'WIO', 'NWC'),
                                   feature_group_count=a.shape[-1])
    return out + bias


def compress_blocks(t, pe, w1, w2):
    b, s, g, dh = t.shape
    chunks = t.reshape(b, s // CMP_STRIDE, CMP_STRIDE, g, dh)
    blocks = jnp.concatenate([chunks[:, :-1], chunks[:, 1:]], axis=2) + pe[None, None, :, None, :]
    flat = blocks.transpose(0, 1, 3, 2, 4).reshape(b, s // CMP_STRIDE - 1, g, CMP_LEN * dh)
    return jax.nn.silu(flat @ w1) @ w2


def nsa_attention(q, kc, vc, ks, vs, kw, vw, gates):
    b, s = q.shape[0], q.shape[1]
    g_, dh = NSA_KV_GROUPS, NSA_HEAD_DIM
    nc = kc.shape[1]
    nsel = s // SEL_BLOCK
    topn = min(SEL_TOPN, nsel)
    slopes = alibi_slopes()
    cstart = jnp.arange(nc) * CMP_STRIDE
    cend = cstart + CMP_LEN - 1
    sel_ids = jnp.arange(nsel)
    overlap = ((cend[:, None] >= sel_ids[None, :] * SEL_BLOCK)
               & (cstart[:, None] < (sel_ids[None, :] + 1) * SEL_BLOCK)).astype(jnp.float32)
    ks_blocks = ks.reshape(b, nsel, SEL_BLOCK, g_, dh).transpose(0, 3, 1, 2, 4)
    vs_blocks = vs.reshape(b, nsel, SEL_BLOCK, g_, dh).transpose(0, 3, 1, 2, 4)
    kw_pad = jnp.pad(kw, ((0, 0), (WINDOW, 0), (0, 0), (0, 0)))
    vw_pad = jnp.pad(vw, ((0, 0), (WINDOW, 0), (0, 0), (0, 0)))
    b_idx = jnp.arange(b)[:, None, None, None]
    g_idx = jnp.arange(g_)[None, :, None, None]
    win_offsets = jnp.arange(WINDOW + Q_BLOCK) - WINDOW
    tok_offsets = jnp.arange(SEL_BLOCK)

    def query_block(qb):
        t0 = qb * Q_BLOCK
        qpos = t0 + jnp.arange(Q_BLOCK)
        qblk = lax.dynamic_slice_in_dim(q, t0, Q_BLOCK, axis=1)
        gblk = lax.dynamic_slice_in_dim(gates, t0, Q_BLOCK, axis=1)
        dist_c = (qpos[:, None] - cend[None, :]).astype(jnp.float32)
        s_c = jnp.einsum('bqgjd,bcgd->bgjqc', qblk, kc).astype(jnp.float32) \
            - slopes[None, :, :, None, None] * dist_c
        p_c = masked_softmax(s_c, dist_c >= 0)
        o_c = jnp.einsum('bgjqc,bcgd->bqgjd', p_c.astype(vc.dtype), vc)
        imp = jnp.einsum('bgjqc,cn->bgqn', p_c, overlap)
        cur = (qpos // SEL_BLOCK)[:, None]
        valid = sel_ids[None, :] <= cur
        forced = (sel_ids[None, :] == 0) | (sel_ids[None, :] == cur) | (sel_ids[None, :] == cur - 1)
        score = jnp.where(valid, jnp.where(forced, FORCE_SCORE, imp), -1.0)
        top_val, top_idx = lax.top_k(score, topn)
        k_sel = ks_blocks[b_idx, g_idx, top_idx].reshape(b, g_, Q_BLOCK, topn * SEL_BLOCK, dh)
        v_sel = vs_blocks[b_idx, g_idx, top_idx].reshape(b, g_, Q_BLOCK, topn * SEL_BLOCK, dh)
        kpos = (top_idx[..., None] * SEL_BLOCK + tok_offsets).reshape(b, g_, Q_BLOCK, topn * SEL_BLOCK)
        blk_ok = jnp.repeat(top_val >= 0, SEL_BLOCK, axis=-1)
        dist_s = qpos[None, None, :, None] - kpos
        mask_s = (blk_ok & (dist_s >= 0))[:, :, None]
        s_s = jnp.einsum('bqgjd,bgqmd->bgjqm', qblk, k_sel).astype(jnp.float32) \
            - slopes[None, :, :, None, None] * dist_s[:, :, None].astype(jnp.float32)
        p_s = masked_softmax(s_s, mask_s)
        o_s = jnp.einsum('bgjqm,bgqmd->bqgjd', p_s.astype(v_sel.dtype), v_sel)
        k_w = lax.dynamic_slice_in_dim(kw_pad, t0, WINDOW + Q_BLOCK, axis=1)
        v_w = lax.dynamic_slice_in_dim(vw_pad, t0, WINDOW + Q_BLOCK, axis=1)
        wpos = t0 + win_offsets
        dist_w = qpos[:, None] - wpos[None, :]
        mask_w = (dist_w >= 0) & (dist_w < WINDOW) & (wpos[None, :] >= 0)
        s_w = jnp.einsum('bqgjd,bkgd->bgjqk', qblk, k_w).astype(jnp.float32) \
            - slopes[None, :, :, None, None] * dist_w.astype(jnp.float32)
        p_w = masked_softmax(s_w, mask_w)
        o_w = jnp.einsum('bgjqk,bkgd->bqgjd', p_w.astype(v_w.dtype), v_w)
        return gblk[..., 0:1] * o_c + gblk[..., 1:2] * o_s + gblk[..., 2:3] * o_w

    out = lax.map(query_block, jnp.arange(s // Q_BLOCK))
    return out.transpose(1, 0, 2, 3, 4, 5).reshape(b, s, NSA_HEADS * dh)


def conv_nsa_mixer(x, w_in, conv_w, conv_b, cln_g, cln_b, pe_k, w1_k, w2_k, pe_v, w1_v, w2_v, w_out):
    b, s, _ = x.shape
    gd = NSA_KV_GROUPS * NSA_HEAD_DIM
    splits = list(np.cumsum([2 * CONV_DIM, NSA_HEADS * NSA_HEAD_DIM, gd, gd, gd, gd, gd, gd]))
    a_in, q, kc, vc, ks, vs, kw, vw, g = jnp.split(x @ w_in, splits, axis=-1)
    a = a_in[..., :CONV_DIM] * jax.nn.sigmoid(a_in[..., CONV_DIM:])
    a = jax.nn.silu(layer_norm(causal_depthwise_conv(a, conv_w, conv_b), cln_g, cln_b))
    kv_shape = (b, s, NSA_KV_GROUPS, NSA_HEAD_DIM)
    q = q.reshape(b, s, NSA_KV_GROUPS, NSA_HPG, NSA_HEAD_DIM) * (NSA_HEAD_DIM ** -0.5)
    kc = compress_blocks(kc.reshape(kv_shape), pe_k, w1_k, w2_k)
    vc = compress_blocks(vc.reshape(kv_shape), pe_v, w1_v, w2_v)
    gates = jax.nn.sigmoid(g).reshape(b, s, NSA_KV_GROUPS, NSA_HPG, 3)
    o = nsa_attention(q, kc, vc, ks.reshape(kv_shape), vs.reshape(kv_shape),
                      kw.reshape(kv_shape), vw.reshape(kv_shape), gates)
    return jnp.concatenate([a, o], axis=-1) @ w_out


def _complex_affine_combine(e1, e2):
    a1r, a1i, b1r, b1i = e1
    a2r, a2i, b2r, b2i = e2
    return (a2r * a1r - a2i * a1i, a2r * a1i + a2i * a1r,
            a2r * b1r - a2i * b1i + b2r, a2r * b1i + a2i * b1r + b2i)


def s5_mixer(x, a_re, a_im, log_dt, b_re, b_im, c_re, c_im, d_skip, w_glu):
    f32 = jnp.float32
    b, s, dm = x.shape
    u = x.astype(f32).reshape(b, s, S5_GROUPS, S5_GROUP)
    dt = jnp.exp(log_dt.astype(f32))[:, None]
    ar, ai = a_re.astype(f32), a_im.astype(f32)
    mag = jnp.exp(ar * dt)
    lr, li = mag * jnp.cos(ai * dt), mag * jnp.sin(ai * dt)
    den = ar * ar + ai * ai
    zr = ((lr - 1.0) * ar + li * ai) / den
    zi = (li * ar - (lr - 1.0) * ai) / den
    br, bi = b_re.astype(f32), b_im.astype(f32)
    bbr = zr[..., None] * br - zi[..., None] * bi
    bbi = zr[..., None] * bi + zi[..., None] * br
    cr, ci = c_re.astype(f32), c_im.astype(f32)
    n_chunks = s // S5_CHUNK
    u_chunks = u.reshape(b, n_chunks, S5_CHUNK, S5_GROUPS, S5_GROUP).transpose(1, 0, 2, 3, 4)

    def step(carry, uc):
        hr, hi = carry
        bur = jnp.einsum('blgc,gpc->blgp', uc, bbr)
        bui = jnp.einsum('blgc,gpc->blgp', uc, bbi)
        pr, pi, sr, si = lax.associative_scan(
            _complex_affine_combine,
            (jnp.broadcast_to(lr, bur.shape), jnp.broadcast_to(li, bur.shape), bur, bui), axis=1)
        sr = sr + pr * hr[:, None] - pi * hi[:, None]
        si = si + pr * hi[:, None] + pi * hr[:, None]
        y = jnp.einsum('blgp,gcp->blgc', sr, cr) - jnp.einsum('blgp,gcp->blgc', si, ci)
        return (sr[:, -1], si[:, -1]), y

    h0 = jnp.zeros((b, S5_GROUPS, S5_STATE), f32)
    _, y = lax.scan(step, (h0, h0), u_chunks)
    y = y.transpose(1, 0, 2, 3, 4).reshape(b, s, dm) + d_skip.astype(f32) * x.astype(f32)
    val, gate = jnp.split(jax.nn.gelu(y).astype(x.dtype) @ w_glu, 2, axis=-1)
    return val * jax.nn.sigmoid(gate)


def setup_inputs(seed: int = 0) -> dict:
    key = jax.random.key(seed)
    keys = iter(jax.random.split(key, 64))
    f32 = jnp.float32

    def nrm(shape, scale):
        return scale * jax.random.normal(next(keys), shape, f32)

    D, F = D_MODEL, FFN_DIM
    dh, gd = NSA_HEAD_DIM, NSA_KV_GROUPS * NSA_HEAD_DIM
    mix_out = CONV_DIM + NSA_HEADS * NSA_HEAD_DIM
    inp = {}
    inp['x'] = nrm((BATCH, SEQ, D), 1.0)
    inp['ffn1_w_in'] = nrm((DEPTH, D, 2 * F), D ** -0.5)
    inp['ffn1_w_out'] = nrm((DEPTH, F, D), DN_BETA * F ** -0.5)
    inp['ffn2_w_in'] = nrm((DEPTH, D, 2 * F), D ** -0.5)
    inp['ffn2_w_out'] = nrm((DEPTH, F, D), DN_BETA * F ** -0.5)
    inp['ln_g'] = 1.0 + nrm((DEPTH, 3, D), 0.02)
    inp['ln_b'] = nrm((DEPTH, 3, D), 0.02)
    inp['ev_w_in'] = nrm((N_EVEN, D, EVEN_IN_COLS), D ** -0.5)
    inp['ev_conv_w'] = nrm((N_EVEN, CONV_WIDTH, CONV_DIM), CONV_WIDTH ** -0.5)
    inp['ev_conv_b'] = nrm((N_EVEN, CONV_DIM), 0.02)
    inp['ev_cln_g'] = 1.0 + nrm((N_EVEN, CONV_DIM), 0.02)
    inp['ev_cln_b'] = nrm((N_EVEN, CONV_DIM), 0.02)
    inp['ev_pe_k'] = nrm((N_EVEN, CMP_LEN, dh), 0.1)
    inp['ev_w1_k'] = nrm((N_EVEN, CMP_LEN * dh, CMP_HIDDEN), (CMP_LEN * dh) ** -0.5)
    inp['ev_w2_k'] = nrm((N_EVEN, CMP_HIDDEN, dh), CMP_HIDDEN ** -0.5)
    inp['ev_pe_v'] = nrm((N_EVEN, CMP_LEN, dh), 0.1)
    inp['ev_w1_v'] = nrm((N_EVEN, CMP_LEN * dh, CMP_HIDDEN), (CMP_LEN * dh) ** -0.5)
    inp['ev_w2_v'] = nrm((N_EVEN, CMP_HIDDEN, dh), CMP_HIDDEN ** -0.5)
    inp['ev_w_out'] = nrm((N_EVEN, mix_out, D), DN_BETA * mix_out ** -0.5)
    n_idx = jnp.arange(S5_STATE, dtype=f32)
    inp['od_a_re'] = -0.5 * (1.0 + nrm((N_ODD, S5_GROUPS, S5_STATE), 0.01))
    inp['od_a_im'] = math.pi * n_idx + nrm((N_ODD, S5_GROUPS, S5_STATE), 0.01)
    inp['od_log_dt'] = jax.random.uniform(next(keys), (N_ODD, S5_GROUPS), f32,
                                          minval=math.log(DT_MIN), maxval=math.log(DT_MAX))
    inp['od_b_re'] = nrm((N_ODD, S5_GROUPS, S5_STATE, S5_GROUP), (2 * S5_GROUP) ** -0.5)
    inp['od_b_im'] = nrm((N_ODD, S5_GROUPS, S5_STATE, S5_GROUP), (2 * S5_GROUP) ** -0.5)
    inp['od_c_re'] = nrm((N_ODD, S5_GROUPS, S5_GROUP, S5_STATE), (2 * S5_STATE) ** -0.5)
    inp['od_c_im'] = nrm((N_ODD, S5_GROUPS, S5_GROUP, S5_STATE), (2 * S5_STATE) ** -0.5)
    inp['od_d'] = nrm((N_ODD, D), 1.0)
    inp['od_w_glu'] = jnp.concatenate([nrm((N_ODD, D, D), DN_BETA * D ** -0.5),
                                       nrm((N_ODD, D, D), D ** -0.5)], axis=-1)
    return inp


def reference(x, ffn1_w_in, ffn1_w_out, ffn2_w_in, ffn2_w_out, ln_g, ln_b,
              ev_w_in, ev_conv_w, ev_conv_b, ev_cln_g, ev_cln_b,
              ev_pe_k, ev_w1_k, ev_w2_k, ev_pe_v, ev_w1_v, ev_w2_v, ev_w_out,
              od_a_re, od_a_im, od_log_dt, od_b_re, od_b_im, od_c_re, od_c_im, od_d, od_w_glu):
    h = x
    for layer in range(DEPTH):
        i = layer // 2
        h = layer_norm(DN_ALPHA * h + 0.5 * swiglu(h, ffn1_w_in[layer], ffn1_w_out[layer]),
                       ln_g[layer, 0], ln_b[layer, 0])
        if layer % 2 == 0:
            m = conv_nsa_mixer(h, ev_w_in[i], ev_conv_w[i], ev_conv_b[i], ev_cln_g[i], ev_cln_b[i],
                               ev_pe_k[i], ev_w1_k[i], ev_w2_k[i], ev_pe_v[i], ev_w1_v[i], ev_w2_v[i],
                               ev_w_out[i])
        else:
            m = s5_mixer(h, od_a_re[i], od_a_im[i], od_log_dt[i], od_b_re[i], od_b_im[i],
                         od_c_re[i], od_c_im[i], od_d[i], od_w_glu[i])
        h = layer_norm(DN_ALPHA * h + m, ln_g[layer, 1], ln_b[layer, 1])
        h = layer_norm(DN_ALPHA * h + 0.5 * swiglu(h, ffn2_w_in[layer], ffn2_w_out[layer]),
                       ln_g[layer, 2], ln_b[layer, 2])
    return h
```

```python
import functools
import math

import jax
import jax.numpy as jnp
from jax import lax
from jax.experimental import pallas as pl
from jax.experimental.pallas import tpu as pltpu

F32 = jnp.float32
BF16 = jnp.bfloat16

D_MODEL = 1024
DEPTH = 4
FFN_DIM = ((8 * D_MODEL // 3 + 127) // 128) * 128
DN_ALPHA = (2.0 * DEPTH) ** 0.25
LN_EPS = 1e-5
CONV_DIM = D_MODEL // 2
CONV_WIDTH = 31
HEAD_DIM = 64
N_HEADS = (D_MODEL // 2) // HEAD_DIM
KV_GROUPS = max(1, N_HEADS // 4)
HPG = N_HEADS // KV_GROUPS
CMP_LEN = 32
CMP_STRIDE = 16
CMP_HIDDEN = 128
SEL_BLOCK = 64
SEL_TOPN = 16
WINDOW = 512
Q_BLOCK = 128
FORCE_SCORE = 1e4
S5_GROUP = 16
S5_GROUPS = D_MODEL // S5_GROUP
S5_STATE = 64
S5_CHUNK = 16
S5_SUPER = 8
N_SUPER = S5_GROUPS // S5_SUPER

LANES = 128
VMEM_LIMIT_BYTES = 56 * 1024 * 1024

NEG_BIG = -1e30
KEY_TILE = 128
QROWS = HPG * Q_BLOCK
WIN_TILES = WINDOW // KEY_TILE + 1


def _cparams(*sem):
    return pltpu.CompilerParams(dimension_semantics=sem, vmem_limit_bytes=VMEM_LIMIT_BYTES)


def _const_spec(shape):
    nd = len(shape)
    return pl.BlockSpec(shape, lambda *_: (0,) * nd)


def _layer_norm(y, g, b):
    mu = jnp.mean(y, axis=-1, keepdims=True)
    yc = y - mu
    var = jnp.mean(yc * yc, axis=-1, keepdims=True)
    return yc * lax.rsqrt(var + LN_EPS) * g + b


FFN_ROWS = 512
FFN_CHUNK = 256


def _ffn_kernel(x_ref, win_ref, wout_ref, g_ref, b_ref, o_ref):
    x = x_ref[...]
    xb = x.astype(BF16)
    acc = None
    for c in range(FFN_DIM // FFN_CHUNK):
        lo = c * FFN_CHUNK
        gate = jnp.dot(xb, win_ref[:, lo:lo + FFN_CHUNK], preferred_element_type=F32)
        val = jnp.dot(xb, win_ref[:, FFN_DIM + lo:FFN_DIM + lo + FFN_CHUNK],
                      preferred_element_type=F32)
        act = (gate * jax.nn.sigmoid(gate) * val).astype(BF16)
        part = jnp.dot(act, wout_ref[lo:lo + FFN_CHUNK, :], preferred_element_type=F32)
        acc = part if acc is None else acc + part
    o_ref[...] = _layer_norm(DN_ALPHA * x + 0.5 * acc, g_ref[...], b_ref[...])


def _ffn_ln(h, w_in, w_out, g, b):
    n = h.shape[0]
    return pl.pallas_call(
        _ffn_kernel,
        out_shape=jax.ShapeDtypeStruct((n, D_MODEL), F32),
        grid=(n // FFN_ROWS,),
        in_specs=[
            pl.BlockSpec((FFN_ROWS, D_MODEL), lambda i: (i, 0)),
            _const_spec((D_MODEL, 2 * FFN_DIM)),
            _const_spec((FFN_DIM, D_MODEL)),
            _const_spec((1, D_MODEL)),
            _const_spec((1, D_MODEL)),
        ],
        out_specs=pl.BlockSpec((FFN_ROWS, D_MODEL), lambda i: (i, 0)),
        compiler_params=_cparams("parallel"),
        name="ffn_ln",
    )(h, w_in.astype(BF16), w_out.astype(BF16), g.reshape(1, -1), b.reshape(1, -1))


PROJ_ROWS = 512
GD = KV_GROUPS * HEAD_DIM
QCOLS = N_HEADS * HEAD_DIM
NAT_COLS = 2 * CONV_DIM + 2 * GD + KV_GROUPS * 2 * HEAD_DIM
GATE_ROWS = 16
TR_ROWS = QCOLS + KV_GROUPS * 2 * HEAD_DIM + KV_GROUPS * GATE_ROWS


def _evproj_kernel(x_ref, wn_ref, wt_ref, a_ref, kcr_ref, vcr_ref, kk_ref, qt_ref, vt_ref, gt_ref):
    xb = x_ref[0].astype(BF16)
    r = jnp.dot(xb, wn_ref[...], preferred_element_type=F32)
    a_ref[0] = r[:, :CONV_DIM] * jax.nn.sigmoid(r[:, CONV_DIM:2 * CONV_DIM])
    off = 2 * CONV_DIM
    kcr_ref[0] = r[:, off:off + GD]
    vcr_ref[0] = r[:, off + GD:off + 2 * GD]
    off += 2 * GD
    for g in range(KV_GROUPS):
        kk_ref[0, g] = r[:, off + g * 128:off + (g + 1) * 128].astype(BF16)
    rt = lax.dot_general(wt_ref[...], xb, (((1,), (1,)), ((), ())), preferred_element_type=F32)
    hq = HPG * HEAD_DIM
    for g in range(KV_GROUPS):
        qt_ref[0, g] = (rt[g * hq:(g + 1) * hq] * (HEAD_DIM ** -0.5)).astype(BF16)
        lo = QCOLS + g * 128
        vt_ref[0, g] = rt[lo:lo + 128].astype(BF16)
        lo = QCOLS + KV_GROUPS * 128 + g * GATE_ROWS
        gt_ref[0, g] = jax.nn.sigmoid(rt[lo:lo + GATE_ROWS])


def _even_proj_weights(w_in):
    gd = GD
    o = 2 * CONV_DIM
    a_in = w_in[:, :o]
    q = w_in[:, o:o + QCOLS]
    o += QCOLS
    kc, vc, ks, vs, kw, vw = [w_in[:, o + i * gd:o + (i + 1) * gd] for i in range(6)]
    gates = w_in[:, o + 6 * gd:]
    nat = [a_in, kc, vc]
    tr = [q]
    for g in range(KV_GROUPS):
        sl = slice(g * HEAD_DIM, (g + 1) * HEAD_DIM)
        nat += [ks[:, sl], kw[:, sl]]
    for g in range(KV_GROUPS):
        sl = slice(g * HEAD_DIM, (g + 1) * HEAD_DIM)
        tr += [vs[:, sl], vw[:, sl]]
    for g in range(KV_GROUPS):
        gg = gates[:, g * 3 * HPG:(g + 1) * 3 * HPG]
        tr.append(jnp.pad(gg, ((0, 0), (0, GATE_ROWS - 3 * HPG))))
    wn = jnp.concatenate(nat, axis=1).astype(BF16)
    wt = jnp.concatenate(tr, axis=1).T.astype(BF16)
    return wn, wt


def _even_proj(h3, w_in):
    b, s, _ = h3.shape
    wn, wt = _even_proj_weights(w_in)
    nt = s // PROJ_ROWS
    rows = PROJ_ROWS
    return pl.pallas_call(
        _evproj_kernel,
        out_shape=[
            jax.ShapeDtypeStruct((b, s, CONV_DIM), F32),
            jax.ShapeDtypeStruct((b, s, GD), F32),
            jax.ShapeDtypeStruct((b, s, GD), F32),
            jax.ShapeDtypeStruct((b, KV_GROUPS, s, 128), BF16),
            jax.ShapeDtypeStruct((b, KV_GROUPS, HPG * HEAD_DIM, s), BF16),
            jax.ShapeDtypeStruct((b, KV_GROUPS, 128, s), BF16),
            jax.ShapeDtypeStruct((b, KV_GROUPS, GATE_ROWS, s), F32),
        ],
        grid=(b, nt),
        in_specs=[
            pl.BlockSpec((1, rows, D_MODEL), lambda i, j: (i, j, 0)),
            _const_spec((D_MODEL, NAT_COLS)),
            _const_spec((TR_ROWS, D_MODEL)),
        ],
        out_specs=[
            pl.BlockSpec((1, rows, CONV_DIM), lambda i, j: (i, j, 0)),
            pl.BlockSpec((1, rows, GD), lambda i, j: (i, j, 0)),
            pl.BlockSpec((1, rows, GD), lambda i, j: (i, j, 0)),
            pl.BlockSpec((1, KV_GROUPS, rows, 128), lambda i, j: (i, 0, j, 0)),
            pl.BlockSpec((1, KV_GROUPS, HPG * HEAD_DIM, rows), lambda i, j: (i, 0, 0, j)),
            pl.BlockSpec((1, KV_GROUPS, 128, rows), lambda i, j: (i, 0, 0, j)),
            pl.BlockSpec((1, KV_GROUPS, GATE_ROWS, rows), lambda i, j: (i, 0, 0, j)),
        ],
        compiler_params=_cparams("parallel", "parallel"),
        name="even_proj",
    )(h3, wn, wt)


CONV_ROWS = 256
CONV_HALO = 32


def _conv_kernel(cur_ref, halo_ref, w_ref, cb_ref, g_ref, b_ref, o_ref, buf_ref):
    first = pl.program_id(1) == 0
    halo = halo_ref[0]
    buf_ref[0:CONV_HALO, :] = jnp.where(first, 0.0, halo)
    buf_ref[CONV_HALO:CONV_HALO + CONV_ROWS, :] = cur_ref[0]
    lead = CONV_HALO - (CONV_WIDTH - 1)
    acc = jnp.zeros((CONV_ROWS, CONV_DIM), F32) + cb_ref[...]
    for k in range(CONV_WIDTH):
        acc = acc + w_ref[k] * buf_ref[lead + k:lead + k + CONV_ROWS, :]
    y = _layer_norm(acc, g_ref[...], b_ref[...])
    o_ref[0] = (y * jax.nn.sigmoid(y)).astype(BF16)


def _conv_module(a, conv_w, conv_b, cln_g, cln_b):
    b, s, _ = a.shape
    per = CONV_ROWS // CONV_HALO
    return pl.pallas_call(
        _conv_kernel,
        out_shape=jax.ShapeDtypeStruct((b, s, CONV_DIM), BF16),
        grid=(b, s // CONV_ROWS),
        in_specs=[
            pl.BlockSpec((1, CONV_ROWS, CONV_DIM), lambda i, j: (i, j, 0)),
            pl.BlockSpec((1, CONV_HALO, CONV_DIM), lambda i, j: (i, jnp.maximum(j * per - 1, 0), 0)),
            _const_spec((CONV_WIDTH, 1, CONV_DIM)),
            _const_spec((1, CONV_DIM)),
            _const_spec((1, CONV_DIM)),
            _const_spec((1, CONV_DIM)),
        ],
        out_specs=pl.BlockSpec((1, CONV_ROWS, CONV_DIM), lambda i, j: (i, j, 0)),
        scratch_shapes=[pltpu.VMEM((CONV_HALO + CONV_ROWS, CONV_DIM), F32)],
        compiler_params=_cparams("parallel", "arbitrary"),
        name="conv_module",
    )(a, a, conv_w.reshape(CONV_WIDTH, 1, CONV_DIM), conv_b.reshape(1, -1),
      cln_g.reshape(1, -1), cln_b.reshape(1, -1))


def _compress_one(x_ref, pe_ref, w1_ref, w2_ref):
    nch = x_ref.shape[1]
    first = None
    second = None
    for l in range(CMP_STRIDE):
        xl = x_ref[0, :, l, :]
        pa = jnp.dot((xl + pe_ref[l]).astype(BF16), w1_ref[l], preferred_element_type=F32)
        pb = jnp.dot((xl + pe_ref[CMP_STRIDE + l]).astype(BF16), w1_ref[CMP_STRIDE + l],
                     preferred_element_type=F32)
        first = pa if first is None else first + pa
        second = pb if second is None else second + pb
    u = first + pltpu.roll(second, nch - 1, 0)
    hid = (u * jax.nn.sigmoid(u)).astype(BF16)
    out = jnp.dot(hid, w2_ref[...], preferred_element_type=F32)
    row = lax.broadcasted_iota(jnp.int32, out.shape, 0)
    return jnp.where(row < nch - 1, out, 0.0)


def _compress_kernel(xk_ref, xv_ref, pek_ref, w1k_ref, w2k_ref, pev_ref, w1v_ref, w2v_ref,
                     kc_ref, vct_ref):
    kc = _compress_one(xk_ref, pek_ref, w1k_ref, w2k_ref)
    vct = _compress_one(xv_ref, pev_ref, w1v_ref, w2v_ref).T
    for g in range(KV_GROUPS):
        kc_ref[0, g] = kc[:, g * HEAD_DIM:(g + 1) * HEAD_DIM].astype(BF16)
        vct_ref[0, g] = vct[g * HEAD_DIM:(g + 1) * HEAD_DIM, :].astype(BF16)


def _compress_weights(pe, w1, w2):
    eye = jnp.eye(KV_GROUPS, dtype=F32)
    pe2 = jnp.tile(pe, (1, KV_GROUPS)).reshape(CMP_LEN, 1, GD)
    w1r = w1.reshape(CMP_LEN, HEAD_DIM, CMP_HIDDEN)
    w1bd = jnp.einsum('ldh,gk->lgdkh', w1r, eye).reshape(CMP_LEN, GD, KV_GROUPS * CMP_HIDDEN)
    w2bd = jnp.einsum('hd,gk->ghkd', w2, eye).reshape(KV_GROUPS * CMP_HIDDEN, GD)
    return pe2, w1bd.astype(BF16), w2bd.astype(BF16)


def _compress(kcr, vcr, pe_k, w1_k, w2_k, pe_v, w1_v, w2_v):
    b, s, _ = kcr.shape
    nch = s // CMP_STRIDE
    xk = kcr.reshape(b, nch, CMP_STRIDE, GD)
    xv = vcr.reshape(b, nch, CMP_STRIDE, GD)
    wk = _compress_weights(pe_k, w1_k, w2_k)
    wv = _compress_weights(pe_v, w1_v, w2_v)
    xspec = pl.BlockSpec((1, nch, CMP_STRIDE, GD), lambda i: (i, 0, 0, 0))
    wspecs = [_const_spec((CMP_LEN, 1, GD)), _const_spec((CMP_LEN, GD, KV_GROUPS * CMP_HIDDEN)),
              _const_spec((KV_GROUPS * CMP_HIDDEN, GD))]
    return pl.pallas_call(
        _compress_kernel,
        out_shape=[jax.ShapeDtypeStruct((b, KV_GROUPS, nch, HEAD_DIM), BF16),
                   jax.ShapeDtypeStruct((b, KV_GROUPS, HEAD_DIM, nch), BF16)],
        grid=(b,),
        in_specs=[xspec, xspec] + wspecs + wspecs,
        out_specs=[pl.BlockSpec((1, KV_GROUPS, nch, HEAD_DIM), lambda i: (i, 0, 0, 0)),
                   pl.BlockSpec((1, KV_GROUPS, HEAD_DIM, nch), lambda i: (i, 0, 0, 0))],
        compiler_params=_cparams("parallel"),
        name="compress",
    )(xk, xv, *wk, *wv)


def _attend_tile(carry, k_tile, v_tile, q_rhs, bias):
    m, l, acc = carry
    s = jnp.dot(k_tile, q_rhs, preferred_element_type=F32) + bias
    m_new = jnp.maximum(m, jnp.max(s, axis=0, keepdims=True))
    p = jnp.exp(s - m_new)
    alpha = jnp.exp(m - m_new)
    l = alpha * l + jnp.sum(p, axis=0, keepdims=True)
    acc = alpha * acc + jnp.dot(v_tile, p.astype(BF16), preferred_element_type=F32)
    return m_new, l, acc


def _nsa_kernel(qt_ref, kc_ref, vct_ref, kk_ref, vt_ref, gt_ref, ovt_ref, tab_ref, slope_ref,
                o_ref, selb_ref):
    qb = pl.program_id(2)
    t0 = qb * Q_BLOCK
    nsel = selb_ref.shape[0]
    qt = qt_ref[0, 0]
    qcat = jnp.concatenate([qt[j * HEAD_DIM:(j + 1) * HEAD_DIM, :] for j in range(HPG)], axis=1)
    zeros_q = jnp.zeros_like(qcat)
    q_sel = jnp.concatenate([qcat, zeros_q], axis=0)
    q_win = jnp.concatenate([zeros_q, qcat], axis=0)
    slope = slope_ref[0]

    nc = kc_ref.shape[2]
    s = jnp.dot(kc_ref[0, 0], qcat, preferred_element_type=F32)
    cend = lax.broadcasted_iota(jnp.int32, (nc, QROWS), 0) * CMP_STRIDE + (CMP_LEN - 1)
    qpos = t0 + (lax.broadcasted_iota(jnp.int32, (nc, QROWS), 1) & (Q_BLOCK - 1))
    dist = (qpos - cend).astype(F32)
    s = jnp.where(dist >= 0, s - slope * dist, -jnp.inf)
    m_c = jnp.maximum(jnp.max(s, axis=0, keepdims=True), NEG_BIG)
    p = jnp.exp(s - m_c)
    l_c = jnp.sum(p, axis=0, keepdims=True)
    pb = p.astype(BF16)
    inv_c = jnp.where(l_c > 0, 1.0 / l_c, 0.0)
    o_cmp = jnp.dot(vct_ref[0, 0], pb, preferred_element_type=F32) * inv_c
    imp4 = jnp.dot(ovt_ref[...], pb, preferred_element_type=F32) * inv_c
    imp = imp4[:, 0:Q_BLOCK]
    for j in range(1, HPG):
        imp = imp + imp4[:, j * Q_BLOCK:(j + 1) * Q_BLOCK]

    n_iota = lax.broadcasted_iota(jnp.int32, (nsel, Q_BLOCK), 0)
    cur = (t0 + lax.broadcasted_iota(jnp.int32, (nsel, Q_BLOCK), 1)) // SEL_BLOCK
    valid = n_iota <= cur
    forced = (n_iota == 0) | (n_iota == cur) | (n_iota == cur - 1)
    score = jnp.where(valid, jnp.where(forced, FORCE_SCORE, imp), -1.0)
    selb = jnp.full((nsel, Q_BLOCK), -jnp.inf, F32)

    def pick(_, carry):
        score, selb = carry
        best = jnp.max(score, axis=0, keepdims=True)
        first = jnp.min(jnp.where(score == best, n_iota, nsel), axis=0, keepdims=True)
        hit = n_iota == first
        selb = jnp.where(hit & (best >= 0), 0.0, selb)
        return jnp.where(hit, -2.0, score), selb

    _, selb = lax.fori_loop(0, min(SEL_TOPN, nsel), pick, (score, selb))
    selb_ref[...] = selb

    init = (jnp.full((1, QROWS), NEG_BIG, F32), jnp.zeros((1, QROWS), F32),
            jnp.zeros((HEAD_DIM, QROWS), F32))

    def sel_tile(t, carry):
        k0 = pl.multiple_of(t * KEY_TILE, KEY_TILE)
        typ = jnp.where(t == qb, 2, 1)
        rows = []
        for i in range(KEY_TILE // SEL_BLOCK):
            r = selb_ref[pl.ds(t * (KEY_TILE // SEL_BLOCK) + i, 1), :]
            rows.append(jnp.broadcast_to(r, (SEL_BLOCK, Q_BLOCK)))
        sb = jnp.concatenate(rows, axis=0)
        bias = tab_ref[0, typ] + slope * (k0 - t0).astype(F32) + jnp.concatenate([sb] * HPG, axis=1)
        return _attend_tile(carry, kk_ref[0, 0, pl.ds(k0, KEY_TILE), :],
                            vt_ref[0, 0, 0:HEAD_DIM, pl.ds(k0, KEY_TILE)], q_sel, bias)

    _, l_s, acc_s = lax.fori_loop(0, qb + 1, sel_tile, init)

    def win_tile(w, carry):
        k0 = pl.multiple_of(t0 - WINDOW + w * KEY_TILE, KEY_TILE)
        typ = jnp.where(w == 0, 0, jnp.where(w == WIN_TILES - 1, 2, 1))
        bias = tab_ref[0, typ] + slope * (k0 - t0).astype(F32)
        return _attend_tile(carry, kk_ref[0, 0, pl.ds(k0, KEY_TILE), :],
                            vt_ref[0, 0, HEAD_DIM:2 * HEAD_DIM, pl.ds(k0, KEY_TILE)], q_win, bias)

    _, l_w, acc_w = lax.fori_loop(jnp.maximum(0, WIN_TILES - 1 - qb), WIN_TILES, win_tile, init)

    gt = gt_ref[0, 0]

    def gate_row(branch):
        return jnp.concatenate([gt[j * 3 + branch:j * 3 + branch + 1, :] for j in range(HPG)], axis=1)

    o_t = gate_row(0) * o_cmp + gate_row(1) * (acc_s / l_s) + gate_row(2) * (acc_w / l_w)
    out = jnp.concatenate([o_t[:, j * Q_BLOCK:(j + 1) * Q_BLOCK].T for j in range(HPG)], axis=1)
    o_ref[0] = out.astype(BF16)


def _nsa_tables(s):
    heads = jnp.arange(1, N_HEADS + 1, dtype=F32)
    slopes = (2.0 ** (-8.0 * heads / N_HEADS)).reshape(KV_GROUPS, HPG)
    slope_rows = jnp.repeat(slopes, Q_BLOCK, axis=1).reshape(KV_GROUPS, 1, QROWS)
    i = jnp.arange(KEY_TILE, dtype=F32)[:, None]
    ql = jnp.tile(jnp.arange(Q_BLOCK, dtype=F32), HPG)[None, :]
    base = slope_rows * (i - ql)[None]
    neg = jnp.float32(-jnp.inf)
    tabs = jnp.stack([jnp.where(i > ql, base, neg), base, jnp.where(i <= ql, base, neg)], axis=1)
    nch = s // CMP_STRIDE
    nsel = s // SEL_BLOCK
    c = jnp.arange(nch)
    n = jnp.arange(nsel)
    ov = ((c[None, :] * CMP_STRIDE + CMP_LEN - 1 >= n[:, None] * SEL_BLOCK)
          & (c[None, :] * CMP_STRIDE < (n[:, None] + 1) * SEL_BLOCK) & (c[None, :] < nch - 1))
    return slope_rows, tabs, ov.astype(BF16)


def _nsa(qt, kc, vct, kk, vt, gt):
    b, _, _, s = qt.shape
    nch = s // CMP_STRIDE
    nsel = s // SEL_BLOCK
    slope_rows, tabs, ovt = _nsa_tables(s)
    hq = HPG * HEAD_DIM
    return pl.pallas_call(
        _nsa_kernel,
        out_shape=jax.ShapeDtypeStruct((b, s, N_HEADS * HEAD_DIM), BF16),
        grid=(b, KV_GROUPS, s // Q_BLOCK),
        in_specs=[
            pl.BlockSpec((1, 1, hq, Q_BLOCK), lambda i, g, q: (i, g, 0, q)),
            pl.BlockSpec((1, 1, nch, HEAD_DIM), lambda i, g, q: (i, g, 0, 0)),
            pl.BlockSpec((1, 1, HEAD_DIM, nch), lambda i, g, q: (i, g, 0, 0)),
            pl.BlockSpec((1, 1, s, 128), lambda i, g, q: (i, g, 0, 0)),
            pl.BlockSpec((1, 1, 128, s), lambda i, g, q: (i, g, 0, 0)),
            pl.BlockSpec((1, 1, GATE_ROWS, Q_BLOCK), lambda i, g, q: (i, g, 0, q)),
            _const_spec((nsel, nch)),
            pl.BlockSpec((1, 3, KEY_TILE, QROWS), lambda i, g, q: (g, 0, 0, 0)),
            pl.BlockSpec((1, 1, QROWS), lambda i, g, q: (g, 0, 0)),
        ],
        out_specs=pl.BlockSpec((1, Q_BLOCK, hq), lambda i, g, q: (i, q, g)),
        scratch_shapes=[pltpu.VMEM((nsel, Q_BLOCK), F32)],
        compiler_params=_cparams("parallel", "parallel", "arbitrary"),
        name="nsa_attention",
    )(qt, kc, vct, kk, vt, gt, ovt, tabs, slope_rows)


OUT_ROWS = 512


def _outproj_kernel(h_ref, a_ref, o_ref, wa_ref, wo_ref, g_ref, b_ref, out_ref):
    m = (jnp.dot(a_ref[...], wa_ref[...], preferred_element_type=F32)
         + jnp.dot(o_ref[...], wo_ref[...], preferred_element_type=F32))
    out_ref[...] = _layer_norm(DN_ALPHA * h_ref[...] + m, g_ref[...], b_ref[...])


def _outproj_ln(h, a, o, w_out, g, b):
    n = h.shape[0]
    wa = w_out[:CONV_DIM].astype(BF16)
    wo = w_out[CONV_DIM:].astype(BF16)
    row = lambda width: pl.BlockSpec((OUT_ROWS, width), lambda i: (i, 0))
    return pl.pallas_call(
        _outproj_kernel,
        out_shape=jax.ShapeDtypeStruct((n, D_MODEL), F32),
        grid=(n // OUT_ROWS,),
        in_specs=[row(D_MODEL), row(CONV_DIM), row(QCOLS),
                  _const_spec((CONV_DIM, D_MODEL)), _const_spec((QCOLS, D_MODEL)),
                  _const_spec((1, D_MODEL)), _const_spec((1, D_MODEL))],
        out_specs=row(D_MODEL),
        compiler_params=_cparams("parallel"),
        name="mixer_out_ln",
    )(h, a, o, wa, wo, g.reshape(1, -1), b.reshape(1, -1))


S5_ROWS = 256
S5_FOLD = S5_CHUNK * LANES
S5_HALF = S5_SUPER * S5_STATE
SCAN_COLS = 512


def _s5_operators(a_re, a_im, log_dt, b_re, b_im, c_re, c_im):
    hi = lax.Precision.HIGHEST
    dt = jnp.exp(log_dt)[:, None]
    ar, ai = a_re, a_im
    mag = jnp.exp(ar * dt)
    lr, li = mag * jnp.cos(ai * dt), mag * jnp.sin(ai * dt)
    den = ar * ar + ai * ai
    zr = ((lr - 1.0) * ar + li * ai) / den
    zi = (li * ar - (lr - 1.0) * ai) / den
    bbr = zr[..., None] * b_re - zi[..., None] * b_im
    bbi = zr[..., None] * b_im + zi[..., None] * b_re
    tau = jnp.arange(S5_CHUNK + 1, dtype=F32)[:, None, None]
    pmag = jnp.exp(ar * dt * tau)
    pr, pi = pmag * jnp.cos(ai * dt * tau), pmag * jnp.sin(ai * dt * tau)
    clr = c_re[None] * pr[:, :, None, :] - c_im[None] * pi[:, :, None, :]
    cli = c_re[None] * pi[:, :, None, :] + c_im[None] * pr[:, :, None, :]
    kern = (jnp.einsum('tgxp,gpc->gtxc', clr[:S5_CHUNK], bbr, precision=hi)
            - jnp.einsum('tgxp,gpc->gtxc', cli[:S5_CHUNK], bbi, precision=hi))
    lin = jnp.arange(S5_CHUNK)
    lag = lin[None, :] - lin[:, None]
    m6 = jnp.where((lag >= 0)[None, :, :, None, None], kern[:, jnp.maximum(lag, 0)], 0.0)
    m6 = m6.transpose(0, 1, 4, 2, 3)
    eye = jnp.eye(S5_SUPER, dtype=F32)
    m8 = m6.reshape(N_SUPER, S5_SUPER, S5_CHUNK, S5_GROUP, S5_CHUNK, S5_GROUP)
    m_op = jnp.einsum('sglcmd,gh->slgcmhd', m8, eye).reshape(N_SUPER, S5_FOLD, S5_FOLD)
    rev = S5_CHUNK - 1 - lin
    ppr = pr[rev][:, :, :, None] * bbr[None] - pi[rev][:, :, :, None] * bbi[None]
    ppi = pr[rev][:, :, :, None] * bbi[None] + pi[rev][:, :, :, None] * bbr[None]
    p2 = jnp.stack([ppr, ppi], axis=0)
    p8 = p2.reshape(2, S5_CHUNK, N_SUPER, S5_SUPER, S5_STATE, S5_GROUP)
    p_op = jnp.einsum('rlsgpc,gh->slgcrhp', p8, eye).reshape(N_SUPER, S5_FOLD, 2 * S5_HALF)
    qr = clr[1:]
    qi = -cli[1:]
    q2 = jnp.stack([qr, qi], axis=0)
    q8 = q2.reshape(2, S5_CHUNK, N_SUPER, S5_SUPER, S5_GROUP, S5_STATE)
    q_op = jnp.einsum('rlsgcp,gh->srgplhc', q8, eye).reshape(N_SUPER, 2 * S5_HALF, S5_FOLD)
    dr = pr[S5_CHUNK].reshape(1, -1)
    di = pi[S5_CHUNK].reshape(1, -1)
    return m_op.astype(BF16), p_op.astype(BF16), q_op.astype(BF16), dr, di


def _fold_rows(x_ref):
    return [x_ref[:, l, :] for l in range(S5_CHUNK)]


def _s5_state_kernel(x_ref, p_ref, vr_ref, vi_ref):
    x2 = jnp.concatenate(_fold_rows(x_ref), axis=1).astype(BF16)
    v = jnp.dot(x2, p_ref[0], preferred_element_type=F32)
    vr_ref[...] = v[:, :S5_HALF]
    vi_ref[...] = v[:, S5_HALF:]


def _s5_scan_kernel(vr_ref, vi_ref, dr_ref, di_ref, hr_ref, hi_ref):
    dr = dr_ref[...]
    di = di_ref[...]

    def body(k, carry):
        hr, hi = carry
        hr_ref[0, pl.ds(k, 1), :] = hr
        hi_ref[0, pl.ds(k, 1), :] = hi
        vr = vr_ref[0, pl.ds(k, 1), :]
        vi = vi_ref[0, pl.ds(k, 1), :]
        return dr * hr - di * hi + vr, dr * hi + di * hr + vi

    zero = jnp.zeros((1, SCAN_COLS), F32)
    lax.fori_loop(0, vr_ref.shape[1], body, (zero, zero))


def _s5_out_kernel(x_ref, hr_ref, hi_ref, m_ref, q_ref, d_ref, y_ref):
    xs = _fold_rows(x_ref)
    x2 = jnp.concatenate(xs, axis=1).astype(BF16)
    hcat = jnp.concatenate([hr_ref[...], hi_ref[...]], axis=1).astype(BF16)
    y = (jnp.dot(x2, m_ref[0], preferred_element_type=F32)
         + jnp.dot(hcat, q_ref[0], preferred_element_type=F32))
    d = d_ref[...]
    for l in range(S5_CHUNK):
        y_ref[:, l, :] = jax.nn.gelu(y[:, l * LANES:(l + 1) * LANES] + d * xs[l])


def _glu_ln_kernel(h_ref, y_ref, w_ref, g_ref, b_ref, o_ref):
    r = jnp.dot(y_ref[...].astype(BF16), w_ref[...], preferred_element_type=F32)
    m = r[:, :D_MODEL] * jax.nn.sigmoid(r[:, D_MODEL:])
    o_ref[...] = _layer_norm(DN_ALPHA * h_ref[...] + m, g_ref[...], b_ref[...])


def _s5_mixer_ln(h, nbatch, params, d_skip, w_glu, g, b):
    n = h.shape[0]
    nrow = n // S5_CHUNK
    rows = min(S5_ROWS, nrow)
    m_op, p_op, q_op, dr, di = _s5_operators(*params)
    x3 = h.reshape(nrow, S5_CHUNK, D_MODEL)
    xspec = pl.BlockSpec((rows, S5_CHUNK, LANES), lambda s, i: (i, 0, s))
    half = pl.BlockSpec((rows, S5_HALF), lambda s, i: (i, s))
    states = N_SUPER * S5_HALF
    vr, vi = pl.pallas_call(
        _s5_state_kernel,
        out_shape=[jax.ShapeDtypeStruct((nrow, states), F32)] * 2,
        grid=(N_SUPER, nrow // rows),
        in_specs=[xspec, pl.BlockSpec((1, S5_FOLD, 2 * S5_HALF), lambda s, i: (s, 0, 0))],
        out_specs=[half, half],
        compiler_params=_cparams("parallel", "parallel"),
        name="s5_state_in",
    )(x3, p_op)
    nch = nrow // nbatch
    seq = pl.BlockSpec((1, nch, SCAN_COLS), lambda i, c: (i, 0, c))
    dspec = pl.BlockSpec((1, SCAN_COLS), lambda i, c: (0, c))
    hr, hi = pl.pallas_call(
        _s5_scan_kernel,
        out_shape=[jax.ShapeDtypeStruct((nbatch, nch, states), F32)] * 2,
        grid=(nbatch, states // SCAN_COLS),
        in_specs=[seq, seq, dspec, dspec],
        out_specs=[seq, seq],
        compiler_params=_cparams("parallel", "parallel"),
        name="s5_scan",
    )(vr.reshape(nbatch, nch, states), vi.reshape(nbatch, nch, states), dr, di)
    y3 = pl.pallas_call(
        _s5_out_kernel,
        out_shape=jax.ShapeDtypeStruct((nrow, S5_CHUNK, D_MODEL), F32),
        grid=(N_SUPER, nrow // rows),
        in_specs=[xspec, half, half,
                  pl.BlockSpec((1, S5_FOLD, S5_FOLD), lambda s, i: (s, 0, 0)),
                  pl.BlockSpec((1, 2 * S5_HALF, S5_FOLD), lambda s, i: (s, 0, 0)),
                  pl.BlockSpec((1, LANES), lambda s, i: (0, s))],
        out_specs=xspec,
        compiler_params=_cparams("parallel", "parallel"),
        name="s5_out",
    )(x3, hr.reshape(nrow, states), hi.reshape(nrow, states), m_op, q_op, d_skip.reshape(1, -1))
    row = lambda width: pl.BlockSpec((OUT_ROWS, width), lambda i: (i, 0))
    return pl.pallas_call(
        _glu_ln_kernel,
        out_shape=jax.ShapeDtypeStruct((n, D_MODEL), F32),
        grid=(n // OUT_ROWS,),
        in_specs=[row(D_MODEL), row(D_MODEL), _const_spec((D_MODEL, 2 * D_MODEL)),
                  _const_spec((1, D_MODEL)), _const_spec((1, D_MODEL))],
        out_specs=row(D_MODEL),
        compiler_params=_cparams("parallel"),
        name="s5_glu_ln",
    )(h, y3.reshape(n, D_MODEL), w_glu.astype(BF16), g.reshape(1, -1), b.reshape(1, -1))


def kernel(x, ffn1_w_in, ffn1_w_out, ffn2_w_in, ffn2_w_out, ln_g, ln_b, ev_w_in, ev_conv_w, ev_conv_b, ev_cln_g, ev_cln_b, ev_pe_k, ev_w1_k, ev_w2_k, ev_pe_v, ev_w1_v, ev_w2_v, ev_w_out, od_a_re, od_a_im, od_log_dt, od_b_re, od_b_im, od_c_re, od_c_im, od_d, od_w_glu):
    bsz, seq, dm = x.shape
    assert dm == D_MODEL and seq % PROJ_ROWS == 0 and seq >= 2 * WINDOW
    n = bsz * seq
    h = x.reshape(n, dm)
    for layer in range(DEPTH):
        i = layer // 2
        h = _ffn_ln(h, ffn1_w_in[layer], ffn1_w_out[layer], ln_g[layer, 0], ln_b[layer, 0])
        if layer % 2 == 0:
            a, kcr, vcr, kk, qt, vt, gt = _even_proj(h.reshape(bsz, seq, dm), ev_w_in[i])
            a = _conv_module(a, ev_conv_w[i], ev_conv_b[i], ev_cln_g[i], ev_cln_b[i])
            kc, vct = _compress(kcr, vcr, ev_pe_k[i], ev_w1_k[i], ev_w2_k[i],
                                ev_pe_v[i], ev_w1_v[i], ev_w2_v[i])
            o = _nsa(qt, kc, vct, kk, vt, gt)
            h = _outproj_ln(h, a.reshape(n, CONV_DIM), o.reshape(n, QCOLS), ev_w_out[i],
                            ln_g[layer, 1], ln_b[layer, 1])
        else:
            params = (od_a_re[i], od_a_im[i], od_log_dt[i], od_b_re[i], od_b_im[i],
                      od_c_re[i], od_c_im[i])
            h = _s5_mixer_ln(h, bsz, params, od_d[i], od_w_glu[i], ln_g[layer, 1], ln_b[layer, 1])
        h = _ffn_ln(h, ffn2_w_in[layer], ffn2_w_out[layer], ln_g[layer, 2], ln_b[layer, 2])
    return h.reshape(bsz, seq, dm)
```

```python
import functools
import math

import jax
import jax.numpy as jnp
from jax import lax
from jax.experimental import pallas as pl
from jax.experimental.pallas import tpu as pltpu

F32 = jnp.float32
BF16 = jnp.bfloat16

D_MODEL = 1024
DEPTH = 4
FFN_DIM = ((8 * D_MODEL // 3 + 127) // 128) * 128
DN_ALPHA = (2.0 * DEPTH) ** 0.25
LN_EPS = 1e-5
CONV_DIM = D_MODEL // 2
CONV_WIDTH = 31
HEAD_DIM = 64
N_HEADS = (D_MODEL // 2) // HEAD_DIM
KV_GROUPS = max(1, N_HEADS // 4)
HPG = N_HEADS // KV_GROUPS
CMP_LEN = 32
CMP_STRIDE = 16
CMP_HIDDEN = 128
SEL_BLOCK = 64
SEL_TOPN = 16
WINDOW = 512
Q_BLOCK = 128
FORCE_SCORE = 1e4
S5_GROUP = 16
S5_GROUPS = D_MODEL // S5_GROUP
S5_STATE = 64
S5_CHUNK = 16
S5_SUPER = 8
N_SUPER = S5_GROUPS // S5_SUPER

LANES = 128
SUBLANES = 8
VMEM_LIMIT_BYTES = 56 * 1024 * 1024

NEG_BIG = -1e30
KEY_TILE = 128
QROWS = HPG * Q_BLOCK
WIN_TILES = WINDOW // KEY_TILE + 1
SEL_SHIFT = SEL_BLOCK.bit_length() - 1
TILE_BLOCKS = KEY_TILE // SEL_BLOCK
WORD_BLOCKS = 16
WORD_SHIFT = WORD_BLOCKS.bit_length() - 1
SEL_UNROLL = 4


def _cparams(*sem):
    return pltpu.CompilerParams(dimension_semantics=sem, vmem_limit_bytes=VMEM_LIMIT_BYTES)


def _const_spec(shape):
    nd = len(shape)
    return pl.BlockSpec(shape, lambda *_: (0,) * nd)


def _layer_norm(y, g, b):
    mu = jnp.mean(y, axis=-1, keepdims=True)
    yc = y - mu
    var = jnp.mean(yc * yc, axis=-1, keepdims=True)
    return yc * lax.rsqrt(var + LN_EPS) * g + b


FFN_ROWS = 512
FFN_CHUNK = 256


def _ffn_kernel(x_ref, win_ref, wout_ref, g_ref, b_ref, o_ref):
    x = x_ref[...]
    xb = x.astype(BF16)
    acc = None
    for c in range(FFN_DIM // FFN_CHUNK):
        lo = c * FFN_CHUNK
        gate = jnp.dot(xb, win_ref[:, lo:lo + FFN_CHUNK], preferred_element_type=F32)
        val = jnp.dot(xb, win_ref[:, FFN_DIM + lo:FFN_DIM + lo + FFN_CHUNK],
                      preferred_element_type=F32)
        act = (gate * jax.nn.sigmoid(gate) * val).astype(BF16)
        part = jnp.dot(act, wout_ref[lo:lo + FFN_CHUNK, :], preferred_element_type=F32)
        acc = part if acc is None else acc + part
    o_ref[...] = _layer_norm(DN_ALPHA * x + 0.5 * acc, g_ref[...], b_ref[...])


def _ffn_ln(h, w_in, w_out, g, b):
    n = h.shape[0]
    return pl.pallas_call(
        _ffn_kernel,
        out_shape=jax.ShapeDtypeStruct((n, D_MODEL), F32),
        grid=(n // FFN_ROWS,),
        in_specs=[
            pl.BlockSpec((FFN_ROWS, D_MODEL), lambda i: (i, 0)),
            _const_spec((D_MODEL, 2 * FFN_DIM)),
            _const_spec((FFN_DIM, D_MODEL)),
            _const_spec((1, D_MODEL)),
            _const_spec((1, D_MODEL)),
        ],
        out_specs=pl.BlockSpec((FFN_ROWS, D_MODEL), lambda i: (i, 0)),
        compiler_params=_cparams("parallel"),
        name="ffn_ln",
    )(h, w_in.astype(BF16), w_out.astype(BF16), g.reshape(1, -1), b.reshape(1, -1))


PROJ_ROWS = 512
GD = KV_GROUPS * HEAD_DIM
QCOLS = N_HEADS * HEAD_DIM
NAT_COLS = 2 * CONV_DIM + 2 * GD + KV_GROUPS * 2 * HEAD_DIM
GATE_ROWS = 16
TR_ROWS = QCOLS + KV_GROUPS * 2 * HEAD_DIM + KV_GROUPS * GATE_ROWS


def _evproj_kernel(x_ref, wn_ref, wt_ref, a_ref, kcr_ref, vcr_ref, kk_ref, qt_ref, vt_ref, gt_ref):
    xb = x_ref[0].astype(BF16)
    r = jnp.dot(xb, wn_ref[...], preferred_element_type=F32)
    a_ref[0] = r[:, :CONV_DIM] * jax.nn.sigmoid(r[:, CONV_DIM:2 * CONV_DIM])
    off = 2 * CONV_DIM
    kcr_ref[0] = r[:, off:off + GD]
    vcr_ref[0] = r[:, off + GD:off + 2 * GD]
    off += 2 * GD
    for g in range(KV_GROUPS):
        kk_ref[0, g] = r[:, off + g * 128:off + (g + 1) * 128].astype(BF16)
    rt = lax.dot_general(wt_ref[...], xb, (((1,), (1,)), ((), ())), preferred_element_type=F32)
    hq = HPG * HEAD_DIM
    for g in range(KV_GROUPS):
        qt_ref[0, g] = (rt[g * hq:(g + 1) * hq] * (HEAD_DIM ** -0.5)).astype(BF16)
        lo = QCOLS + g * 128
        vt_ref[0, g] = rt[lo:lo + 128].astype(BF16)
        lo = QCOLS + KV_GROUPS * 128 + g * GATE_ROWS
        gt_ref[0, g] = jax.nn.sigmoid(rt[lo:lo + GATE_ROWS])


def _even_proj_weights(w_in):
    gd = GD
    o = 2 * CONV_DIM
    a_in = w_in[:, :o]
    q = w_in[:, o:o + QCOLS]
    o += QCOLS
    kc, vc, ks, vs, kw, vw = [w_in[:, o + i * gd:o + (i + 1) * gd] for i in range(6)]
    gates = w_in[:, o + 6 * gd:]
    nat = [a_in, kc, vc]
    tr = [q]
    for g in range(KV_GROUPS):
        sl = slice(g * HEAD_DIM, (g + 1) * HEAD_DIM)
        nat += [ks[:, sl], kw[:, sl]]
    for g in range(KV_GROUPS):
        sl = slice(g * HEAD_DIM, (g + 1) * HEAD_DIM)
        tr += [vs[:, sl], vw[:, sl]]
    for g in range(KV_GROUPS):
        gg = gates[:, g * 3 * HPG:(g + 1) * 3 * HPG]
        tr.append(jnp.pad(gg, ((0, 0), (0, GATE_ROWS - 3 * HPG))))
    wn = jnp.concatenate(nat, axis=1).astype(BF16)
    wt = jnp.concatenate(tr, axis=1).T.astype(BF16)
    return wn, wt


def _even_proj(h3, w_in):
    b, s, _ = h3.shape
    wn, wt = _even_proj_weights(w_in)
    nt = s // PROJ_ROWS
    rows = PROJ_ROWS
    return pl.pallas_call(
        _evproj_kernel,
        out_shape=[
            jax.ShapeDtypeStruct((b, s, CONV_DIM), F32),
            jax.ShapeDtypeStruct((b, s, GD), F32),
            jax.ShapeDtypeStruct((b, s, GD), F32),
            jax.ShapeDtypeStruct((b, KV_GROUPS, s, 128), BF16),
            jax.ShapeDtypeStruct((b, KV_GROUPS, HPG * HEAD_DIM, s), BF16),
            jax.ShapeDtypeStruct((b, KV_GROUPS, 128, s), BF16),
            jax.ShapeDtypeStruct((b, KV_GROUPS, GATE_ROWS, s), F32),
        ],
        grid=(b, nt),
        in_specs=[
            pl.BlockSpec((1, rows, D_MODEL), lambda i, j: (i, j, 0)),
            _const_spec((D_MODEL, NAT_COLS)),
            _const_spec((TR_ROWS, D_MODEL)),
        ],
        out_specs=[
            pl.BlockSpec((1, rows, CONV_DIM), lambda i, j: (i, j, 0)),
            pl.BlockSpec((1, rows, GD), lambda i, j: (i, j, 0)),
            pl.BlockSpec((1, rows, GD), lambda i, j: (i, j, 0)),
            pl.BlockSpec((1, KV_GROUPS, rows, 128), lambda i, j: (i, 0, j, 0)),
            pl.BlockSpec((1, KV_GROUPS, HPG * HEAD_DIM, rows), lambda i, j: (i, 0, 0, j)),
            pl.BlockSpec((1, KV_GROUPS, 128, rows), lambda i, j: (i, 0, 0, j)),
            pl.BlockSpec((1, KV_GROUPS, GATE_ROWS, rows), lambda i, j: (i, 0, 0, j)),
        ],
        compiler_params=_cparams("parallel", "parallel"),
        name="even_proj",
    )(h3, wn, wt)


CONV_ROWS = 256
CONV_HALO = 32
CONV_SUB = 128


def _conv_kernel(cur_ref, halo_ref, w_ref, cb_ref, g_ref, b_ref, o_ref, buf_ref, acc_ref, sh_ref):
    first = pl.program_id(1) == 0
    buf_ref[0:CONV_HALO, :] = jnp.where(first, 0.0, halo_ref[0])
    buf_ref[CONV_HALO:CONV_HALO + CONV_ROWS, :] = cur_ref[0]
    lead = CONV_HALO - (CONV_WIDTH - 1)
    for c in range(CONV_DIM // LANES):
        cols = slice(c * LANES, (c + 1) * LANES)
        for rc in range(CONV_ROWS // CONV_SUB):
            acc = jnp.zeros((CONV_SUB, LANES), F32) + cb_ref[:, cols]
            for r in range(SUBLANES):
                base = lead + r + rc * CONV_SUB
                taps = range(r, CONV_WIDTH, SUBLANES)
                span = CONV_SUB + (len(taps) - 1) * SUBLANES
                sh_ref[0:span, :] = buf_ref[base:base + span, cols]
                for k in taps:
                    acc = acc + w_ref[k][:, cols] * sh_ref[k - r:k - r + CONV_SUB, :]
            acc_ref[rc * CONV_SUB:(rc + 1) * CONV_SUB, cols] = acc
    y = _layer_norm(acc_ref[...], g_ref[...], b_ref[...])
    o_ref[0] = (y * jax.nn.sigmoid(y)).astype(BF16)


def _conv_module(a, conv_w, conv_b, cln_g, cln_b):
    b, s, _ = a.shape
    per = CONV_ROWS // CONV_HALO
    return pl.pallas_call(
        _conv_kernel,
        out_shape=jax.ShapeDtypeStruct((b, s, CONV_DIM), BF16),
        grid=(b, s // CONV_ROWS),
        in_specs=[
            pl.BlockSpec((1, CONV_ROWS, CONV_DIM), lambda i, j: (i, j, 0)),
            pl.BlockSpec((1, CONV_HALO, CONV_DIM), lambda i, j: (i, jnp.maximum(j * per - 1, 0), 0)),
            _const_spec((CONV_WIDTH, 1, CONV_DIM)),
            _const_spec((1, CONV_DIM)),
            _const_spec((1, CONV_DIM)),
            _const_spec((1, CONV_DIM)),
        ],
        out_specs=pl.BlockSpec((1, CONV_ROWS, CONV_DIM), lambda i, j: (i, j, 0)),
        scratch_shapes=[pltpu.VMEM((CONV_HALO + CONV_ROWS, CONV_DIM), F32),
                        pltpu.VMEM((CONV_ROWS, CONV_DIM), F32),
                        pltpu.VMEM((CONV_SUB + (-(-CONV_WIDTH // SUBLANES) - 1) * SUBLANES, LANES), F32)],
        compiler_params=_cparams("parallel", "arbitrary"),
        name="conv_module",
    )(a, a, conv_w.reshape(CONV_WIDTH, 1, CONV_DIM), conv_b.reshape(1, -1),
      cln_g.reshape(1, -1), cln_b.reshape(1, -1))


def _compress_one(x_ref, pe_ref, w1_ref, w2_ref):
    nch = x_ref.shape[1]
    first = None
    second = None
    for l in range(CMP_STRIDE):
        xl = x_ref[0, :, l, :]
        pa = jnp.dot((xl + pe_ref[l]).astype(BF16), w1_ref[l], preferred_element_type=F32)
        pb = jnp.dot((xl + pe_ref[CMP_STRIDE + l]).astype(BF16), w1_ref[CMP_STRIDE + l],
                     preferred_element_type=F32)
        first = pa if first is None else first + pa
        second = pb if second is None else second + pb
    u = first + pltpu.roll(second, nch - 1, 0)
    hid = (u * jax.nn.sigmoid(u)).astype(BF16)
    out = jnp.dot(hid, w2_ref[...], preferred_element_type=F32)
    row = lax.broadcasted_iota(jnp.int32, out.shape, 0)
    return jnp.where(row < nch - 1, out, 0.0)


def _compress_kernel(xk_ref, xv_ref, pek_ref, w1k_ref, w2k_ref, pev_ref, w1v_ref, w2v_ref,
                     kc_ref, vct_ref):
    kc = _compress_one(xk_ref, pek_ref, w1k_ref, w2k_ref)
    vct = _compress_one(xv_ref, pev_ref, w1v_ref, w2v_ref).T
    for g in range(KV_GROUPS):
        kc_ref[0, g] = kc[:, g * HEAD_DIM:(g + 1) * HEAD_DIM].astype(BF16)
        vct_ref[0, g] = vct[g * HEAD_DIM:(g + 1) * HEAD_DIM, :].astype(BF16)


def _compress_weights(pe, w1, w2):
    eye = jnp.eye(KV_GROUPS, dtype=F32)
    pe2 = jnp.tile(pe, (1, KV_GROUPS)).reshape(CMP_LEN, 1, GD)
    w1r = w1.reshape(CMP_LEN, HEAD_DIM, CMP_HIDDEN)
    w1bd = jnp.einsum('ldh,gk->lgdkh', w1r, eye).reshape(CMP_LEN, GD, KV_GROUPS * CMP_HIDDEN)
    w2bd = jnp.einsum('hd,gk->ghkd', w2, eye).reshape(KV_GROUPS * CMP_HIDDEN, GD)
    return pe2, w1bd.astype(BF16), w2bd.astype(BF16)


def _compress(kcr, vcr, pe_k, w1_k, w2_k, pe_v, w1_v, w2_v):
    b, s, _ = kcr.shape
    nch = s // CMP_STRIDE
    xk = kcr.reshape(b, nch, CMP_STRIDE, GD)
    xv = vcr.reshape(b, nch, CMP_STRIDE, GD)
    wk = _compress_weights(pe_k, w1_k, w2_k)
    wv = _compress_weights(pe_v, w1_v, w2_v)
    xspec = pl.BlockSpec((1, nch, CMP_STRIDE, GD), lambda i: (i, 0, 0, 0))
    wspecs = [_const_spec((CMP_LEN, 1, GD)), _const_spec((CMP_LEN, GD, KV_GROUPS * CMP_HIDDEN)),
              _const_spec((KV_GROUPS * CMP_HIDDEN, GD))]
    return pl.pallas_call(
        _compress_kernel,
        out_shape=[jax.ShapeDtypeStruct((b, KV_GROUPS, nch, HEAD_DIM), BF16),
                   jax.ShapeDtypeStruct((b, KV_GROUPS, HEAD_DIM, nch), BF16)],
        grid=(b,),
        in_specs=[xspec, xspec] + wspecs + wspecs,
        out_specs=[pl.BlockSpec((1, KV_GROUPS, nch, HEAD_DIM), lambda i: (i, 0, 0, 0)),
                   pl.BlockSpec((1, KV_GROUPS, HEAD_DIM, nch), lambda i: (i, 0, 0, 0))],
        compiler_params=_cparams("parallel"),
        name="compress",
    )(xk, xv, *wk, *wv)


def _attend_step(carry, tiles, q_rhs):
    m, l, acc = carry
    scores = [jnp.dot(k, q_rhs, preferred_element_type=F32) + bias for k, _, bias in tiles]
    m_new = m
    for s in scores:
        m_new = jnp.maximum(m_new, jnp.max(s, axis=0, keepdims=True))
    probs = [jnp.exp(s - m_new) for s in scores]
    alpha = jnp.exp(m - m_new)
    l = alpha * l
    for p in probs:
        l = l + jnp.sum(p, axis=0, keepdims=True)
    p_all = jnp.concatenate([p.astype(BF16) for p in probs], axis=0)
    v_all = jnp.concatenate([v for _, v, _ in tiles], axis=1)
    acc = alpha * acc + jnp.dot(v_all, p_all, preferred_element_type=F32)
    return m_new, l, acc


def _nsa_kernel(qt_ref, kc_ref, vct_ref, kk_ref, vt_ref, gt_ref, ovt_ref, tab_ref, slope_ref,
                o_ref, selb_ref, words_ref, list_ref):
    qb = pl.program_id(2)
    t0 = qb * Q_BLOCK
    nsel = selb_ref.shape[0]
    qt = qt_ref[0, 0]
    qcat = jnp.concatenate([qt[j * HEAD_DIM:(j + 1) * HEAD_DIM, :] for j in range(HPG)], axis=1)
    zeros_q = jnp.zeros_like(qcat)
    q_sel = jnp.concatenate([qcat, zeros_q], axis=0)
    q_win = jnp.concatenate([zeros_q, qcat], axis=0)
    slope = slope_ref[0]

    nc = kc_ref.shape[2]
    s = jnp.dot(kc_ref[0, 0], qcat, preferred_element_type=F32)
    cend = lax.broadcasted_iota(jnp.int32, (nc, QROWS), 0) * CMP_STRIDE + (CMP_LEN - 1)
    qpos = t0 + (lax.broadcasted_iota(jnp.int32, (nc, QROWS), 1) & (Q_BLOCK - 1))
    dist = (qpos - cend).astype(F32)
    s = jnp.where(dist >= 0, s - slope * dist, -jnp.inf)
    m_c = jnp.maximum(jnp.max(s, axis=0, keepdims=True), NEG_BIG)
    p = jnp.exp(s - m_c)
    l_c = jnp.sum(p, axis=0, keepdims=True)
    pb = p.astype(BF16)
    inv_c = jnp.where(l_c > 0, 1.0 / l_c, 0.0)
    o_cmp = jnp.dot(vct_ref[0, 0], pb, preferred_element_type=F32) * inv_c
    imp4 = jnp.dot(ovt_ref[...], pb, preferred_element_type=F32) * inv_c
    imp = imp4[:, 0:Q_BLOCK]
    for j in range(1, HPG):
        imp = imp + imp4[:, j * Q_BLOCK:(j + 1) * Q_BLOCK]

    n_iota = lax.broadcasted_iota(jnp.int32, (nsel, Q_BLOCK), 0)
    cur = (t0 + lax.broadcasted_iota(jnp.int32, (nsel, Q_BLOCK), 1)) >> SEL_SHIFT
    valid = n_iota <= cur
    forced = (n_iota == 0) | (n_iota == cur) | (n_iota == cur - 1)
    score = jnp.where(valid, jnp.where(forced, FORCE_SCORE, imp), -1.0)
    selb = jnp.full((nsel, Q_BLOCK), -jnp.inf, F32)

    def pick(_, carry):
        score, selb = carry
        best = jnp.max(score, axis=0, keepdims=True)
        first = jnp.min(jnp.where(score == best, n_iota, nsel), axis=0, keepdims=True)
        hit = n_iota == first
        selb = jnp.where(hit & (best >= 0), 0.0, selb)
        return jnp.where(hit, -2.0, score), selb

    _, selb = lax.fori_loop(0, min(SEL_TOPN, nsel), pick, (score, selb))
    selb_ref[...] = selb

    active = jnp.max(jnp.where(selb == 0.0, 1.0, 0.0), axis=1, keepdims=True)
    weight = jnp.left_shift(1, lax.broadcasted_iota(jnp.int32, (nsel, 1), 0) & (WORD_BLOCKS - 1))
    packed = active * weight.astype(F32)
    for w in range(nsel // WORD_BLOCKS):
        words_ref[w] = jnp.sum(packed[w * WORD_BLOCKS:(w + 1) * WORD_BLOCKS, :]).astype(jnp.int32)

    def compact(t, n):
        blk = t * TILE_BLOCKS
        word = words_ref[blk >> WORD_SHIFT]
        hit = ((word >> (blk & (WORD_BLOCKS - 1))) & ((1 << TILE_BLOCKS) - 1)) != 0
        list_ref[n] = t
        return n + hit.astype(jnp.int32)

    n_act = lax.fori_loop(0, qb + 1, compact, 0)
    for u in range(SEL_UNROLL):
        list_ref[n_act + u] = -1

    init = (jnp.full((1, QROWS), NEG_BIG, F32), jnp.zeros((1, QROWS), F32),
            jnp.zeros((HEAD_DIM, QROWS), F32))

    def key_tile(t_raw, diag_typ, v_lo):
        null = t_raw < 0
        k0 = pl.multiple_of(jnp.maximum(t_raw, 0) * KEY_TILE, KEY_TILE)
        typ = jnp.where(null, 3, diag_typ)
        bias = tab_ref[0, typ] + slope * (k0 - t0).astype(F32)
        return k0, (kk_ref[0, 0, pl.ds(k0, KEY_TILE), :],
                    vt_ref[0, 0, v_lo:v_lo + HEAD_DIM, pl.ds(k0, KEY_TILE)], bias)

    def sel_step(i, carry):
        tiles = []
        for u in range(SEL_UNROLL):
            t_raw = list_ref[i * SEL_UNROLL + u]
            k0, (k, v, bias) = key_tile(t_raw, jnp.where(t_raw == qb, 2, 1), 0)
            rows = []
            for r in range(TILE_BLOCKS):
                row = selb_ref[pl.ds((k0 >> SEL_SHIFT) + r, 1), :]
                rows.append(jnp.broadcast_to(row, (SEL_BLOCK, Q_BLOCK)))
            sb = jnp.concatenate(rows, axis=0)
            tiles.append((k, v, bias + jnp.concatenate([sb] * HPG, axis=1)))
        return _attend_step(carry, tiles, q_sel)

    n_steps = (n_act + SEL_UNROLL - 1) >> (SEL_UNROLL.bit_length() - 1)
    _, l_s, acc_s = lax.fori_loop(0, n_steps, sel_step, init)

    tiles = []
    for w in range(WIN_TILES):
        t_raw = qb - (WIN_TILES - 1) + w
        typ = 0 if w == 0 else (2 if w == WIN_TILES - 1 else 1)
        tiles.append(key_tile(t_raw, typ, HEAD_DIM)[1])
    _, l_w, acc_w = _attend_step(init, tiles, q_win)

    gt = gt_ref[0, 0]

    def gate_row(branch):
        return jnp.concatenate([gt[j * 3 + branch:j * 3 + branch + 1, :] for j in range(HPG)], axis=1)

    o_t = gate_row(0) * o_cmp + gate_row(1) * (acc_s / l_s) + gate_row(2) * (acc_w / l_w)
    out = jnp.concatenate([o_t[:, j * Q_BLOCK:(j + 1) * Q_BLOCK].T for j in range(HPG)], axis=1)
    o_ref[0] = out.astype(BF16)


def _nsa_tables(s):
    heads = jnp.arange(1, N_HEADS + 1, dtype=F32)
    slopes = (2.0 ** (-8.0 * heads / N_HEADS)).reshape(KV_GROUPS, HPG)
    slope_rows = jnp.repeat(slopes, Q_BLOCK, axis=1).reshape(KV_GROUPS, 1, QROWS)
    i = jnp.arange(KEY_TILE, dtype=F32)[:, None]
    ql = jnp.tile(jnp.arange(Q_BLOCK, dtype=F32), HPG)[None, :]
    base = slope_rows * (i - ql)[None]
    neg = jnp.float32(-jnp.inf)
    tabs = jnp.stack([jnp.where(i > ql, base, neg), base, jnp.where(i <= ql, base, neg),
                      jnp.full_like(base, neg)], axis=1)
    nch = s // CMP_STRIDE
    nsel = s // SEL_BLOCK
    c = jnp.arange(nch)
    n = jnp.arange(nsel)
    ov = ((c[None, :] * CMP_STRIDE + CMP_LEN - 1 >= n[:, None] * SEL_BLOCK)
          & (c[None, :] * CMP_STRIDE < (n[:, None] + 1) * SEL_BLOCK) & (c[None, :] < nch - 1))
    return slope_rows, tabs, ov.astype(BF16)


def _nsa(qt, kc, vct, kk, vt, gt):
    b, _, _, s = qt.shape
    nch = s // CMP_STRIDE
    nsel = s // SEL_BLOCK
    slope_rows, tabs, ovt = _nsa_tables(s)
    hq = HPG * HEAD_DIM
    return pl.pallas_call(
        _nsa_kernel,
        out_shape=jax.ShapeDtypeStruct((b, s, N_HEADS * HEAD_DIM), BF16),
        grid=(b, KV_GROUPS, s // Q_BLOCK),
        in_specs=[
            pl.BlockSpec((1, 1, hq, Q_BLOCK), lambda i, g, q: (i, g, 0, q)),
            pl.BlockSpec((1, 1, nch, HEAD_DIM), lambda i, g, q: (i, g, 0, 0)),
            pl.BlockSpec((1, 1, HEAD_DIM, nch), lambda i, g, q: (i, g, 0, 0)),
            pl.BlockSpec((1, 1, s, 128), lambda i, g, q: (i, g, 0, 0)),
            pl.BlockSpec((1, 1, 128, s), lambda i, g, q: (i, g, 0, 0)),
            pl.BlockSpec((1, 1, GATE_ROWS, Q_BLOCK), lambda i, g, q: (i, g, 0, q)),
            _const_spec((nsel, nch)),
            pl.BlockSpec((1, 4, KEY_TILE, QROWS), lambda i, g, q: (g, 0, 0, 0)),
            pl.BlockSpec((1, 1, QROWS), lambda i, g, q: (g, 0, 0)),
        ],
        out_specs=pl.BlockSpec((1, Q_BLOCK, hq), lambda i, g, q: (i, q, g)),
        scratch_shapes=[pltpu.VMEM((nsel, Q_BLOCK), F32),
                        pltpu.SMEM((max(nsel // WORD_BLOCKS, 1),), jnp.int32),
                        pltpu.SMEM((s // KEY_TILE + SEL_UNROLL,), jnp.int32)],
        compiler_params=_cparams("parallel", "parallel", "arbitrary"),
        name="nsa_attention",
    )(qt, kc, vct, kk, vt, gt, ovt, tabs, slope_rows)


OUT_ROWS = 512


def _outproj_kernel(h_ref, a_ref, o_ref, wa_ref, wo_ref, g_ref, b_ref, out_ref):
    m = (jnp.dot(a_ref[...], wa_ref[...], preferred_element_type=F32)
         + jnp.dot(o_ref[...], wo_ref[...], preferred_element_type=F32))
    out_ref[...] = _layer_norm(DN_ALPHA * h_ref[...] + m, g_ref[...], b_ref[...])


def _outproj_ln(h, a, o, w_out, g, b):
    n = h.shape[0]
    wa = w_out[:CONV_DIM].astype(BF16)
    wo = w_out[CONV_DIM:].astype(BF16)
    row = lambda width: pl.BlockSpec((OUT_ROWS, width), lambda i: (i, 0))
    return pl.pallas_call(
        _outproj_kernel,
        out_shape=jax.ShapeDtypeStruct((n, D_MODEL), F32),
        grid=(n // OUT_ROWS,),
        in_specs=[row(D_MODEL), row(CONV_DIM), row(QCOLS),
                  _const_spec((CONV_DIM, D_MODEL)), _const_spec((QCOLS, D_MODEL)),
                  _const_spec((1, D_MODEL)), _const_spec((1, D_MODEL))],
        out_specs=row(D_MODEL),
        compiler_params=_cparams("parallel"),
        name="mixer_out_ln",
    )(h, a, o, wa, wo, g.reshape(1, -1), b.reshape(1, -1))


S5_ROWS = 256
S5_FOLD = S5_CHUNK * LANES
S5_HALF = S5_SUPER * S5_STATE
SCAN_COLS = 512


def _s5_operators(a_re, a_im, log_dt, b_re, b_im, c_re, c_im):
    hi = lax.Precision.HIGHEST
    dt = jnp.exp(log_dt)[:, None]
    ar, ai = a_re, a_im
    mag = jnp.exp(ar * dt)
    lr, li = mag * jnp.cos(ai * dt), mag * jnp.sin(ai * dt)
    den = ar * ar + ai * ai
    zr = ((lr - 1.0) * ar + li * ai) / den
    zi = (li * ar - (lr - 1.0) * ai) / den
    bbr = zr[..., None] * b_re - zi[..., None] * b_im
    bbi = zr[..., None] * b_im + zi[..., None] * b_re
    tau = jnp.arange(S5_CHUNK + 1, dtype=F32)[:, None, None]
    pmag = jnp.exp(ar * dt * tau)
    pr, pi = pmag * jnp.cos(ai * dt * tau), pmag * jnp.sin(ai * dt * tau)
    clr = c_re[None] * pr[:, :, None, :] - c_im[None] * pi[:, :, None, :]
    cli = c_re[None] * pi[:, :, None, :] + c_im[None] * pr[:, :, None, :]
    kern = (jnp.einsum('tgxp,gpc->gtxc', clr[:S5_CHUNK], bbr, precision=hi)
            - jnp.einsum('tgxp,gpc->gtxc', cli[:S5_CHUNK], bbi, precision=hi))
    eye = jnp.eye(S5_SUPER, dtype=F32)
    k8 = kern.reshape(N_SUPER, S5_SUPER, S5_CHUNK, S5_GROUP, S5_GROUP)
    bd = (k8.transpose(0, 2, 1, 4, 3)[:, :, :, :, None, :] * eye[None, None, :, None, :, None])
    bd = bd.reshape(N_SUPER, S5_CHUNK, LANES, LANES)
    rev = S5_CHUNK - 1 - jnp.arange(S5_CHUNK)
    ppr = pr[rev][:, :, :, None] * bbr[None] - pi[rev][:, :, :, None] * bbi[None]
    ppi = pr[rev][:, :, :, None] * bbi[None] + pi[rev][:, :, :, None] * bbr[None]
    p8 = jnp.stack([ppr, ppi], axis=0).reshape(2, S5_CHUNK, N_SUPER, S5_SUPER, S5_STATE, S5_GROUP)
    pt = p8.transpose(2, 1, 3, 5, 0, 4)
    p_op = (pt[:, :, :, :, :, None, :] * eye[None, None, :, None, None, :, None])
    p_op = p_op.reshape(N_SUPER, S5_FOLD, 2 * S5_HALF)
    q8 = jnp.stack([clr[1:], -cli[1:]], axis=0).reshape(2, S5_CHUNK, N_SUPER, S5_SUPER, S5_GROUP, S5_STATE)
    qt = q8.transpose(2, 0, 3, 5, 1, 4)
    q_op = (qt[:, :, :, :, :, None, :] * eye[None, None, :, None, None, :, None])
    q_op = q_op.reshape(N_SUPER, 2 * S5_HALF, S5_FOLD)
    dr = pr[S5_CHUNK].reshape(1, -1)
    di = pi[S5_CHUNK].reshape(1, -1)
    return bd.astype(BF16), p_op.astype(BF16), q_op.astype(BF16), dr, di


def _fold_rows(x_ref):
    return [x_ref[:, l, :] for l in range(S5_CHUNK)]


def _s5_state_kernel(x_ref, p_ref, vr_ref, vi_ref):
    x2 = jnp.concatenate(_fold_rows(x_ref), axis=1).astype(BF16)
    v = jnp.dot(x2, p_ref[0], preferred_element_type=F32)
    vr_ref[...] = v[:, :S5_HALF]
    vi_ref[...] = v[:, S5_HALF:]


def _s5_scan_kernel(vr_ref, vi_ref, dr_ref, di_ref, hr_ref, hi_ref):
    dr = dr_ref[...]
    di = di_ref[...]

    def body(k, carry):
        hr, hi = carry
        hr_ref[0, pl.ds(k, 1), :] = hr
        hi_ref[0, pl.ds(k, 1), :] = hi
        vr = vr_ref[0, pl.ds(k, 1), :]
        vi = vi_ref[0, pl.ds(k, 1), :]
        return dr * hr - di * hi + vr, dr * hi + di * hr + vi

    zero = jnp.zeros((1, SCAN_COLS), F32)
    lax.fori_loop(0, vr_ref.shape[1], body, (zero, zero))


def _s5_out_kernel(x_ref, hr_ref, hi_ref, bd_ref, q_ref, d_ref, y_ref, m_ref):
    @pl.when(pl.program_id(1) == 0)
    def _():
        zero = jnp.zeros((LANES, LANES), m_ref.dtype)
        for l_in in range(S5_CHUNK):
            for l_out in range(S5_CHUNK):
                blk = bd_ref[0, l_out - l_in] if l_out >= l_in else zero
                m_ref[l_in * LANES:(l_in + 1) * LANES, l_out * LANES:(l_out + 1) * LANES] = blk

    xs = _fold_rows(x_ref)
    x2 = jnp.concatenate(xs, axis=1).astype(BF16)
    hcat = jnp.concatenate([hr_ref[...], hi_ref[...]], axis=1).astype(BF16)
    y = (jnp.dot(x2, m_ref[...], preferred_element_type=F32)
         + jnp.dot(hcat, q_ref[0], preferred_element_type=F32))
    d = d_ref[...]
    for l in range(S5_CHUNK):
        y_ref[:, l, :] = jax.nn.gelu(y[:, l * LANES:(l + 1) * LANES] + d * xs[l])


def _glu_ln_kernel(h_ref, y_ref, w_ref, g_ref, b_ref, o_ref):
    r = jnp.dot(y_ref[...].astype(BF16), w_ref[...], preferred_element_type=F32)
    m = r[:, :D_MODEL] * jax.nn.sigmoid(r[:, D_MODEL:])
    o_ref[...] = _layer_norm(DN_ALPHA * h_ref[...] + m, g_ref[...], b_ref[...])


def _s5_mixer_ln(h, nbatch, params, d_skip, w_glu, g, b):
    n = h.shape[0]
    nrow = n // S5_CHUNK
    rows = min(S5_ROWS, nrow)
    bd_op, p_op, q_op, dr, di = _s5_operators(*params)
    x3 = h.reshape(nrow, S5_CHUNK, D_MODEL)
    xspec = pl.BlockSpec((rows, S5_CHUNK, LANES), lambda s, i: (i, 0, s))
    half = pl.BlockSpec((rows, S5_HALF), lambda s, i: (i, s))
    states = N_SUPER * S5_HALF
    vr, vi = pl.pallas_call(
        _s5_state_kernel,
        out_shape=[jax.ShapeDtypeStruct((nrow, states), F32)] * 2,
        grid=(N_SUPER, nrow // rows),
        in_specs=[xspec, pl.BlockSpec((1, S5_FOLD, 2 * S5_HALF), lambda s, i: (s, 0, 0))],
        out_specs=[half, half],
        compiler_params=_cparams("parallel", "parallel"),
        name="s5_state_in",
    )(x3, p_op)
    nch = nrow // nbatch
    seq = pl.BlockSpec((1, nch, SCAN_COLS), lambda i, c: (i, 0, c))
    dspec = pl.BlockSpec((1, SCAN_COLS), lambda i, c: (0, c))
    hr, hi = pl.pallas_call(
        _s5_scan_kernel,
        out_shape=[jax.ShapeDtypeStruct((nbatch, nch, states), F32)] * 2,
        grid=(nbatch, states // SCAN_COLS),
        in_specs=[seq, seq, dspec, dspec],
        out_specs=[seq, seq],
        compiler_params=_cparams("parallel", "parallel"),
        name="s5_scan",
    )(vr.reshape(nbatch, nch, states), vi.reshape(nbatch, nch, states), dr, di)
    y3 = pl.pallas_call(
        _s5_out_kernel,
        out_shape=jax.ShapeDtypeStruct((nrow, S5_CHUNK, D_MODEL), F32),
        grid=(N_SUPER, nrow // rows),
        in_specs=[xspec, half, half,
                  pl.BlockSpec((1, S5_CHUNK, LANES, LANES), lambda s, i: (s, 0, 0, 0)),
                  pl.BlockSpec((1, 2 * S5_HALF, S5_FOLD), lambda s, i: (s, 0, 0)),
                  pl.BlockSpec((1, LANES), lambda s, i: (0, s))],
        out_specs=xspec,
        scratch_shapes=[pltpu.VMEM((S5_FOLD, S5_FOLD), BF16)],
        compiler_params=_cparams("arbitrary", "arbitrary"),
        name="s5_out",
    )(x3, hr.reshape(nrow, states), hi.reshape(nrow, states), bd_op, q_op, d_skip.reshape(1, -1))
    row = lambda width: pl.BlockSpec((OUT_ROWS, width), lambda i: (i, 0))
    return pl.pallas_call(
        _glu_ln_kernel,
        out_shape=jax.ShapeDtypeStruct((n, D_MODEL), F32),
        grid=(n // OUT_ROWS,),
        in_specs=[row(D_MODEL), row(D_MODEL), _const_spec((D_MODEL, 2 * D_MODEL)),
                  _const_spec((1, D_MODEL)), _const_spec((1, D_MODEL))],
        out_specs=row(D_MODEL),
        compiler_params=_cparams("parallel"),
        name="s5_glu_ln",
    )(h, y3.reshape(n, D_MODEL), w_glu.astype(BF16), g.reshape(1, -1), b.reshape(1, -1))


def kernel(x, ffn1_w_in, ffn1_w_out, ffn2_w_in, ffn2_w_out, ln_g, ln_b, ev_w_in, ev_conv_w, ev_conv_b, ev_cln_g, ev_cln_b, ev_pe_k, ev_w1_k, ev_w2_k, ev_pe_v, ev_w1_v, ev_w2_v, ev_w_out, od_a_re, od_a_im, od_log_dt, od_b_re, od_b_im, od_c_re, od_c_im, od_d, od_w_glu):
    bsz, seq, dm = x.shape
    assert dm == D_MODEL and seq % PROJ_ROWS == 0 and seq >= 2 * WINDOW
    n = bsz * seq
    h = x.reshape(n, dm)
    for layer in range(DEPTH):
        i = layer // 2
        h = _ffn_ln(h, ffn1_w_in[layer], ffn1_w_out[layer], ln_g[layer, 0], ln_b[layer, 0])
        if layer % 2 == 0:
            a, kcr, vcr, kk, qt, vt, gt = _even_proj(h.reshape(bsz, seq, dm), ev_w_in[i])
            a = _conv_module(a, ev_conv_w[i], ev_conv_b[i], ev_cln_g[i], ev_cln_b[i])
            kc, vct = _compress(kcr, vcr, ev_pe_k[i], ev_w1_k[i], ev_w2_k[i],
                                ev_pe_v[i], ev_w1_v[i], ev_w2_v[i])
            o = _nsa(qt, kc, vct, kk, vt, gt)
            h = _outproj_ln(h, a.reshape(n, CONV_DIM), o.reshape(n, QCOLS), ev_w_out[i],
                            ln_g[layer, 1], ln_b[layer, 1])
        else:
            params = (od_a_re[i], od_a_im[i], od_log_dt[i], od_b_re[i], od_b_im[i],
                      od_c_re[i], od_c_im[i])
            h = _s5_mixer_ln(h, bsz, params, od_d[i], od_w_glu[i], ln_g[layer, 1], ln_b[layer, 1])
        h = _ffn_ln(h, ffn2_w_in[layer], ffn2_w_out[layer], ln_g[layer, 2], ln_b[layer, 2])
    return h.reshape(bsz, seq, dm)
```

```python
import functools
import math

import jax
import jax.numpy as jnp
import numpy as np
from jax import lax
from jax.experimental import pallas as pl
from jax.experimental.pallas import tpu as pltpu

F32 = jnp.float32
BF16 = jnp.bfloat16

D_MODEL = 1024
DEPTH = 4
FFN_DIM = ((8 * D_MODEL // 3 + 127) // 128) * 128
DN_ALPHA = (2.0 * DEPTH) ** 0.25
LN_EPS = 1e-5
CONV_DIM = D_MODEL // 2
CONV_WIDTH = 31
HEAD_DIM = 64
N_HEADS = (D_MODEL // 2) // HEAD_DIM
KV_GROUPS = max(1, N_HEADS // 4)
HPG = N_HEADS // KV_GROUPS
CMP_LEN = 32
CMP_STRIDE = 16
CMP_HIDDEN = 128
SEL_BLOCK = 64
SEL_TOPN = 16
WINDOW = 512
Q_BLOCK = 128
FORCE_SCORE = 1e4
S5_GROUP = 16
S5_GROUPS = D_MODEL // S5_GROUP
S5_STATE = 64
S5_CHUNK = 16
S5_SUPER = 8
N_SUPER = S5_GROUPS // S5_SUPER

LANES = 128
SUBLANES = 8
VMEM_LIMIT_BYTES = 56 * 1024 * 1024

NEG_BIG = -1e30
KEY_TILE = 128
QROWS = HPG * Q_BLOCK
WIN_TILES = WINDOW // KEY_TILE + 1
SEL_SHIFT = SEL_BLOCK.bit_length() - 1
TILE_BLOCKS = KEY_TILE // SEL_BLOCK
WORD_BLOCKS = 16
WORD_SHIFT = WORD_BLOCKS.bit_length() - 1
SEL_UNROLL = 8


def _cparams(*sem):
    return pltpu.CompilerParams(dimension_semantics=sem, vmem_limit_bytes=VMEM_LIMIT_BYTES)


def _const_spec(shape):
    nd = len(shape)
    return pl.BlockSpec(shape, lambda *_: (0,) * nd)


def _layer_norm(y, g, b):
    mu = jnp.mean(y, axis=-1, keepdims=True)
    yc = y - mu
    var = jnp.mean(yc * yc, axis=-1, keepdims=True)
    return yc * lax.rsqrt(var + LN_EPS) * g + b


FFN_ROWS = 512
FFN_CHUNK = 256


def _ffn_kernel(x_ref, win_ref, wout_ref, g_ref, b_ref, o_ref):
    x = x_ref[...]
    xb = x.astype(BF16)
    acc = None
    for c in range(FFN_DIM // FFN_CHUNK):
        lo = c * FFN_CHUNK
        gate = jnp.dot(xb, win_ref[:, lo:lo + FFN_CHUNK], preferred_element_type=F32)
        val = jnp.dot(xb, win_ref[:, FFN_DIM + lo:FFN_DIM + lo + FFN_CHUNK],
                      preferred_element_type=F32)
        act = (gate * jax.nn.sigmoid(gate) * val).astype(BF16)
        part = jnp.dot(act, wout_ref[lo:lo + FFN_CHUNK, :], preferred_element_type=F32)
        acc = part if acc is None else acc + part
    o_ref[...] = _layer_norm(DN_ALPHA * x + 0.5 * acc, g_ref[...], b_ref[...])


def _ffn_ln(h, w_in, w_out, g, b):
    n = h.shape[0]
    return pl.pallas_call(
        _ffn_kernel,
        out_shape=jax.ShapeDtypeStruct((n, D_MODEL), F32),
        grid=(n // FFN_ROWS,),
        in_specs=[
            pl.BlockSpec((FFN_ROWS, D_MODEL), lambda i: (i, 0)),
            _const_spec((D_MODEL, 2 * FFN_DIM)),
            _const_spec((FFN_DIM, D_MODEL)),
            _const_spec((1, D_MODEL)),
            _const_spec((1, D_MODEL)),
        ],
        out_specs=pl.BlockSpec((FFN_ROWS, D_MODEL), lambda i: (i, 0)),
        compiler_params=_cparams("parallel"),
        name="ffn_ln",
    )(h, w_in.astype(BF16), w_out.astype(BF16), g.reshape(1, -1), b.reshape(1, -1))


PROJ_ROWS = 512
GD = KV_GROUPS * HEAD_DIM
QCOLS = N_HEADS * HEAD_DIM
NAT_COLS = 2 * CONV_DIM + 2 * GD + KV_GROUPS * 2 * HEAD_DIM
GATE_ROWS = 16
TR_ROWS = QCOLS + KV_GROUPS * 2 * HEAD_DIM + KV_GROUPS * GATE_ROWS


def _evproj_kernel(x_ref, wn_ref, wt_ref, a_ref, kcr_ref, vcr_ref, kk_ref, qt_ref, vt_ref, gt_ref):
    xb = x_ref[0].astype(BF16)
    r = jnp.dot(xb, wn_ref[...], preferred_element_type=F32)
    a_ref[0] = r[:, :CONV_DIM] * jax.nn.sigmoid(r[:, CONV_DIM:2 * CONV_DIM])
    off = 2 * CONV_DIM
    kcr_ref[0] = r[:, off:off + GD]
    vcr_ref[0] = r[:, off + GD:off + 2 * GD]
    off += 2 * GD
    for g in range(KV_GROUPS):
        kk_ref[0, g] = r[:, off + g * 128:off + (g + 1) * 128].astype(BF16)
    rt = lax.dot_general(wt_ref[...], xb, (((1,), (1,)), ((), ())), preferred_element_type=F32)
    hq = HPG * HEAD_DIM
    for g in range(KV_GROUPS):
        qt_ref[0, g] = (rt[g * hq:(g + 1) * hq] * (HEAD_DIM ** -0.5)).astype(BF16)
        lo = QCOLS + g * 128
        vt_ref[0, g] = rt[lo:lo + 128].astype(BF16)
        lo = QCOLS + KV_GROUPS * 128 + g * GATE_ROWS
        gt_ref[0, g] = jax.nn.sigmoid(rt[lo:lo + GATE_ROWS])


def _even_proj_weights(w_in):
    gd = GD
    o = 2 * CONV_DIM
    a_in = w_in[:, :o]
    q = w_in[:, o:o + QCOLS]
    o += QCOLS
    kc, vc, ks, vs, kw, vw = [w_in[:, o + i * gd:o + (i + 1) * gd] for i in range(6)]
    gates = w_in[:, o + 6 * gd:]
    nat = [a_in, kc, vc]
    tr = [q]
    for g in range(KV_GROUPS):
        sl = slice(g * HEAD_DIM, (g + 1) * HEAD_DIM)
        nat += [ks[:, sl], kw[:, sl]]
    for g in range(KV_GROUPS):
        sl = slice(g * HEAD_DIM, (g + 1) * HEAD_DIM)
        tr += [vs[:, sl], vw[:, sl]]
    for g in range(KV_GROUPS):
        gg = gates[:, g * 3 * HPG:(g + 1) * 3 * HPG]
        tr.append(jnp.pad(gg, ((0, 0), (0, GATE_ROWS - 3 * HPG))))
    wn = jnp.concatenate(nat, axis=1).astype(BF16)
    wt = jnp.concatenate(tr, axis=1).T.astype(BF16)
    return wn, wt


def _even_proj(h3, w_in):
    b, s, _ = h3.shape
    wn, wt = _even_proj_weights(w_in)
    nt = s // PROJ_ROWS
    rows = PROJ_ROWS
    return pl.pallas_call(
        _evproj_kernel,
        out_shape=[
            jax.ShapeDtypeStruct((b, s, CONV_DIM), F32),
            jax.ShapeDtypeStruct((b, s, GD), F32),
            jax.ShapeDtypeStruct((b, s, GD), F32),
            jax.ShapeDtypeStruct((b, KV_GROUPS, s, 128), BF16),
            jax.ShapeDtypeStruct((b, KV_GROUPS, HPG * HEAD_DIM, s), BF16),
            jax.ShapeDtypeStruct((b, KV_GROUPS, 128, s), BF16),
            jax.ShapeDtypeStruct((b, KV_GROUPS, GATE_ROWS, s), F32),
        ],
        grid=(b, nt),
        in_specs=[
            pl.BlockSpec((1, rows, D_MODEL), lambda i, j: (i, j, 0)),
            _const_spec((D_MODEL, NAT_COLS)),
            _const_spec((TR_ROWS, D_MODEL)),
        ],
        out_specs=[
            pl.BlockSpec((1, rows, CONV_DIM), lambda i, j: (i, j, 0)),
            pl.BlockSpec((1, rows, GD), lambda i, j: (i, j, 0)),
            pl.BlockSpec((1, rows, GD), lambda i, j: (i, j, 0)),
            pl.BlockSpec((1, KV_GROUPS, rows, 128), lambda i, j: (i, 0, j, 0)),
            pl.BlockSpec((1, KV_GROUPS, HPG * HEAD_DIM, rows), lambda i, j: (i, 0, 0, j)),
            pl.BlockSpec((1, KV_GROUPS, 128, rows), lambda i, j: (i, 0, 0, j)),
            pl.BlockSpec((1, KV_GROUPS, GATE_ROWS, rows), lambda i, j: (i, 0, 0, j)),
        ],
        compiler_params=_cparams("parallel", "parallel"),
        name="even_proj",
    )(h3, wn, wt)


CONV_ROWS = 256
CONV_HALO = 32
CONV_SUB = 128


def _conv_kernel(cur_ref, halo_ref, w_ref, cb_ref, g_ref, b_ref, o_ref, buf_ref, acc_ref, sh_ref):
    first = pl.program_id(1) == 0
    buf_ref[0:CONV_HALO, :] = jnp.where(first, 0.0, halo_ref[0])
    buf_ref[CONV_HALO:CONV_HALO + CONV_ROWS, :] = cur_ref[0]
    lead = CONV_HALO - (CONV_WIDTH - 1)
    for c in range(CONV_DIM // LANES):
        cols = slice(c * LANES, (c + 1) * LANES)
        for rc in range(CONV_ROWS // CONV_SUB):
            acc = jnp.zeros((CONV_SUB, LANES), F32) + cb_ref[:, cols]
            for r in range(SUBLANES):
                base = lead + r + rc * CONV_SUB
                taps = range(r, CONV_WIDTH, SUBLANES)
                span = CONV_SUB + (len(taps) - 1) * SUBLANES
                sh_ref[0:span, :] = buf_ref[base:base + span, cols]
                for k in taps:
                    acc = acc + w_ref[k][:, cols] * sh_ref[k - r:k - r + CONV_SUB, :]
            acc_ref[rc * CONV_SUB:(rc + 1) * CONV_SUB, cols] = acc
    y = _layer_norm(acc_ref[...], g_ref[...], b_ref[...])
    o_ref[0] = (y * jax.nn.sigmoid(y)).astype(BF16)


def _conv_module(a, conv_w, conv_b, cln_g, cln_b):
    b, s, _ = a.shape
    per = CONV_ROWS // CONV_HALO
    return pl.pallas_call(
        _conv_kernel,
        out_shape=jax.ShapeDtypeStruct((b, s, CONV_DIM), BF16),
        grid=(b, s // CONV_ROWS),
        in_specs=[
            pl.BlockSpec((1, CONV_ROWS, CONV_DIM), lambda i, j: (i, j, 0)),
            pl.BlockSpec((1, CONV_HALO, CONV_DIM), lambda i, j: (i, jnp.maximum(j * per - 1, 0), 0)),
            _const_spec((CONV_WIDTH, 1, CONV_DIM)),
            _const_spec((1, CONV_DIM)),
            _const_spec((1, CONV_DIM)),
            _const_spec((1, CONV_DIM)),
        ],
        out_specs=pl.BlockSpec((1, CONV_ROWS, CONV_DIM), lambda i, j: (i, j, 0)),
        scratch_shapes=[pltpu.VMEM((CONV_HALO + CONV_ROWS, CONV_DIM), F32),
                        pltpu.VMEM((CONV_ROWS, CONV_DIM), F32),
                        pltpu.VMEM((CONV_SUB + (-(-CONV_WIDTH // SUBLANES) - 1) * SUBLANES, LANES), F32)],
        compiler_params=_cparams("parallel", "arbitrary"),
        name="conv_module",
    )(a, a, conv_w.reshape(CONV_WIDTH, 1, CONV_DIM), conv_b.reshape(1, -1),
      cln_g.reshape(1, -1), cln_b.reshape(1, -1))


def _compress_one(x_ref, pe_ref, w1_ref, w2_ref):
    nch = x_ref.shape[1]
    first = None
    second = None
    for l in range(CMP_STRIDE):
        xl = x_ref[0, :, l, :]
        pa = jnp.dot((xl + pe_ref[l]).astype(BF16), w1_ref[l], preferred_element_type=F32)
        pb = jnp.dot((xl + pe_ref[CMP_STRIDE + l]).astype(BF16), w1_ref[CMP_STRIDE + l],
                     preferred_element_type=F32)
        first = pa if first is None else first + pa
        second = pb if second is None else second + pb
    u = first + pltpu.roll(second, nch - 1, 0)
    hid = (u * jax.nn.sigmoid(u)).astype(BF16)
    out = jnp.dot(hid, w2_ref[...], preferred_element_type=F32)
    row = lax.broadcasted_iota(jnp.int32, out.shape, 0)
    return jnp.where(row < nch - 1, out, 0.0)


def _compress_kernel(xk_ref, xv_ref, pek_ref, w1k_ref, w2k_ref, pev_ref, w1v_ref, w2v_ref,
                     kc_ref, vct_ref):
    kc = _compress_one(xk_ref, pek_ref, w1k_ref, w2k_ref)
    vct = _compress_one(xv_ref, pev_ref, w1v_ref, w2v_ref).T
    for g in range(KV_GROUPS):
        kc_ref[0, g] = kc[:, g * HEAD_DIM:(g + 1) * HEAD_DIM].astype(BF16)
        vct_ref[0, g] = vct[g * HEAD_DIM:(g + 1) * HEAD_DIM, :].astype(BF16)


def _compress_weights(pe, w1, w2):
    eye = jnp.eye(KV_GROUPS, dtype=F32)
    pe2 = jnp.tile(pe, (1, KV_GROUPS)).reshape(CMP_LEN, 1, GD)
    w1r = w1.reshape(CMP_LEN, HEAD_DIM, CMP_HIDDEN)
    w1bd = jnp.einsum('ldh,gk->lgdkh', w1r, eye).reshape(CMP_LEN, GD, KV_GROUPS * CMP_HIDDEN)
    w2bd = jnp.einsum('hd,gk->ghkd', w2, eye).reshape(KV_GROUPS * CMP_HIDDEN, GD)
    return pe2, w1bd.astype(BF16), w2bd.astype(BF16)


def _compress(kcr, vcr, pe_k, w1_k, w2_k, pe_v, w1_v, w2_v):
    b, s, _ = kcr.shape
    nch = s // CMP_STRIDE
    xk = kcr.reshape(b, nch, CMP_STRIDE, GD)
    xv = vcr.reshape(b, nch, CMP_STRIDE, GD)
    wk = _compress_weights(pe_k, w1_k, w2_k)
    wv = _compress_weights(pe_v, w1_v, w2_v)
    xspec = pl.BlockSpec((1, nch, CMP_STRIDE, GD), lambda i: (i, 0, 0, 0))
    wspecs = [_const_spec((CMP_LEN, 1, GD)), _const_spec((CMP_LEN, GD, KV_GROUPS * CMP_HIDDEN)),
              _const_spec((KV_GROUPS * CMP_HIDDEN, GD))]
    return pl.pallas_call(
        _compress_kernel,
        out_shape=[jax.ShapeDtypeStruct((b, KV_GROUPS, nch, HEAD_DIM), BF16),
                   jax.ShapeDtypeStruct((b, KV_GROUPS, HEAD_DIM, nch), BF16)],
        grid=(b,),
        in_specs=[xspec, xspec] + wspecs + wspecs,
        out_specs=[pl.BlockSpec((1, KV_GROUPS, nch, HEAD_DIM), lambda i: (i, 0, 0, 0)),
                   pl.BlockSpec((1, KV_GROUPS, HEAD_DIM, nch), lambda i: (i, 0, 0, 0))],
        compiler_params=_cparams("parallel"),
        name="compress",
    )(xk, xv, *wk, *wv)


def _attend_step(carry, tiles, q_rhs):
    m, l, acc = carry
    scores = [jnp.dot(k, q_rhs, preferred_element_type=F32) + bias for k, _, bias in tiles]
    m_new = m
    for s in scores:
        m_new = jnp.maximum(m_new, jnp.max(s, axis=0, keepdims=True))
    probs = [jnp.exp(s - m_new) for s in scores]
    alpha = jnp.exp(m - m_new)
    l = alpha * l
    for p in probs:
        l = l + jnp.sum(p, axis=0, keepdims=True)
    p_all = jnp.concatenate([p.astype(BF16) for p in probs], axis=0)
    v_all = jnp.concatenate([v for _, v, _ in tiles], axis=1)
    acc = alpha * acc + jnp.dot(v_all, p_all, preferred_element_type=F32)
    return m_new, l, acc


def _nsa_kernel(qt_ref, kc_ref, vct_ref, kk_ref, vt_ref, gt_ref, ovt_ref, tab_ref, slope_ref,
                o_ref, selb_ref, words_ref, list_ref):
    qb = pl.program_id(2)
    t0 = qb * Q_BLOCK
    nsel = selb_ref.shape[0]
    qt = qt_ref[0, 0]
    qcat = jnp.concatenate([qt[j * HEAD_DIM:(j + 1) * HEAD_DIM, :] for j in range(HPG)], axis=1)
    zeros_q = jnp.zeros_like(qcat)
    q_sel = jnp.concatenate([qcat, zeros_q], axis=0)
    q_win = jnp.concatenate([zeros_q, qcat], axis=0)
    slope = slope_ref[0]

    nc = kc_ref.shape[2]
    s = jnp.dot(kc_ref[0, 0], qcat, preferred_element_type=F32)
    cend = lax.broadcasted_iota(jnp.int32, (nc, QROWS), 0) * CMP_STRIDE + (CMP_LEN - 1)
    qpos = t0 + (lax.broadcasted_iota(jnp.int32, (nc, QROWS), 1) & (Q_BLOCK - 1))
    dist = (qpos - cend).astype(F32)
    s = jnp.where(dist >= 0, s - slope * dist, -jnp.inf)
    m_c = jnp.maximum(jnp.max(s, axis=0, keepdims=True), NEG_BIG)
    p = jnp.exp(s - m_c)
    l_c = jnp.sum(p, axis=0, keepdims=True)
    pb = p.astype(BF16)
    inv_c = jnp.where(l_c > 0, 1.0 / l_c, 0.0)
    o_cmp = jnp.dot(vct_ref[0, 0], pb, preferred_element_type=F32) * inv_c
    imp4 = jnp.dot(ovt_ref[...], pb, preferred_element_type=F32) * inv_c
    imp = imp4[:, 0:Q_BLOCK]
    for j in range(1, HPG):
        imp = imp + imp4[:, j * Q_BLOCK:(j + 1) * Q_BLOCK]

    init = (jnp.full((1, QROWS), NEG_BIG, F32), jnp.zeros((1, QROWS), F32),
            jnp.zeros((HEAD_DIM, QROWS), F32))

    def key_tile(t_raw, diag_typ, v_lo):
        null = t_raw < 0
        k0 = pl.multiple_of(jnp.maximum(t_raw, 0) * KEY_TILE, KEY_TILE)
        typ = jnp.where(null, 3, diag_typ)
        bias = tab_ref[0, typ] + slope * (k0 - t0).astype(F32)
        return k0, (kk_ref[0, 0, pl.ds(k0, KEY_TILE), :],
                    vt_ref[0, 0, v_lo:v_lo + HEAD_DIM, pl.ds(k0, KEY_TILE)], bias)

    tiles = []
    for w in range(WIN_TILES):
        t_raw = qb - (WIN_TILES - 1) + w
        typ = 0 if w == 0 else (2 if w == WIN_TILES - 1 else 1)
        tiles.append(key_tile(t_raw, typ, HEAD_DIM)[1])
    _, l_w, acc_w = _attend_step(init, tiles, q_win)

    n_iota = lax.broadcasted_iota(jnp.int32, (nsel, Q_BLOCK), 0)
    cur = (t0 + lax.broadcasted_iota(jnp.int32, (nsel, Q_BLOCK), 1)) >> SEL_SHIFT
    valid = n_iota <= cur
    forced = (n_iota == 0) | (n_iota == cur) | (n_iota == cur - 1)
    score = jnp.where(valid, jnp.where(forced, FORCE_SCORE, imp), -1.0)
    selb = jnp.full((nsel, Q_BLOCK), -jnp.inf, F32)

    def pick(_, carry):
        score, selb = carry
        best = jnp.max(score, axis=0, keepdims=True)
        first = jnp.min(jnp.where(score == best, n_iota, nsel), axis=0, keepdims=True)
        hit = n_iota == first
        selb = jnp.where(hit & (best >= 0), 0.0, selb)
        return jnp.where(hit, -2.0, score), selb

    carry = (score, selb)
    for r in range(min(SEL_TOPN, nsel)):
        carry = pick(r, carry)
    selb = carry[1]
    selb_ref[...] = selb

    active = jnp.max(jnp.where(selb == 0.0, 1.0, 0.0), axis=1, keepdims=True)
    weight = jnp.left_shift(1, lax.broadcasted_iota(jnp.int32, (nsel, 1), 0) & (WORD_BLOCKS - 1))
    packed = active * weight.astype(F32)
    for w in range(nsel // WORD_BLOCKS):
        words_ref[w] = jnp.sum(packed[w * WORD_BLOCKS:(w + 1) * WORD_BLOCKS, :]).astype(jnp.int32)

    tiles_per_word = WORD_BLOCKS // TILE_BLOCKS

    def compact(w, n):
        word = words_ref[w]
        for u in range(tiles_per_word):
            t = w * tiles_per_word + u
            hit = (((word >> (u * TILE_BLOCKS)) & ((1 << TILE_BLOCKS) - 1)) != 0) & (t <= qb)
            list_ref[n] = t
            n = n + hit.astype(jnp.int32)
        return n

    n_act = lax.fori_loop(0, (qb >> (tiles_per_word.bit_length() - 1)) + 1, compact, 0)
    for u in range(SEL_UNROLL):
        list_ref[n_act + u] = -1

    def sel_step(i, carry):
        tiles = []
        for u in range(SEL_UNROLL):
            t_raw = list_ref[i * SEL_UNROLL + u]
            k0, (k, v, bias) = key_tile(t_raw, jnp.where(t_raw == qb, 2, 1), 0)
            rows = []
            for r in range(TILE_BLOCKS):
                row = selb_ref[pl.ds((k0 >> SEL_SHIFT) + r, 1), :]
                rows.append(jnp.broadcast_to(row, (SEL_BLOCK, Q_BLOCK)))
            sb = jnp.concatenate(rows, axis=0)
            tiles.append((k, v, bias + jnp.concatenate([sb] * HPG, axis=1)))
        return _attend_step(carry, tiles, q_sel)

    n_steps = (n_act + SEL_UNROLL - 1) >> (SEL_UNROLL.bit_length() - 1)
    _, l_s, acc_s = lax.fori_loop(0, n_steps, sel_step, init)

    gt = gt_ref[0, 0]

    def gate_row(branch):
        return jnp.concatenate([gt[j * 3 + branch:j * 3 + branch + 1, :] for j in range(HPG)], axis=1)

    o_t = gate_row(0) * o_cmp + gate_row(1) * (acc_s / l_s) + gate_row(2) * (acc_w / l_w)
    out = jnp.concatenate([o_t[:, j * Q_BLOCK:(j + 1) * Q_BLOCK].T for j in range(HPG)], axis=1)
    o_ref[0] = out.astype(BF16)


def _nsa_tables(s):
    heads = np.arange(1, N_HEADS + 1, dtype=np.float32)
    slopes = (2.0 ** (-8.0 * heads / N_HEADS)).astype(np.float32).reshape(KV_GROUPS, HPG)
    slope_rows = np.repeat(slopes, Q_BLOCK, axis=1).reshape(KV_GROUPS, 1, QROWS)
    i = np.arange(KEY_TILE, dtype=np.float32)[:, None]
    ql = np.tile(np.arange(Q_BLOCK, dtype=np.float32), HPG)[None, :]
    base = slope_rows * (i - ql)[None]
    neg = np.float32(-np.inf)
    tabs = np.stack([np.where(i > ql, base, neg), base, np.where(i <= ql, base, neg),
                     np.full_like(base, neg)], axis=1)
    nch = s // CMP_STRIDE
    nsel = s // SEL_BLOCK
    c = np.arange(nch)
    n = np.arange(nsel)
    ov = ((c[None, :] * CMP_STRIDE + CMP_LEN - 1 >= n[:, None] * SEL_BLOCK)
          & (c[None, :] * CMP_STRIDE < (n[:, None] + 1) * SEL_BLOCK) & (c[None, :] < nch - 1))
    return jnp.asarray(slope_rows), jnp.asarray(tabs), jnp.asarray(ov, BF16)


def _nsa(qt, kc, vct, kk, vt, gt):
    b, _, _, s = qt.shape
    nch = s // CMP_STRIDE
    nsel = s // SEL_BLOCK
    slope_rows, tabs, ovt = _nsa_tables(s)
    hq = HPG * HEAD_DIM
    return pl.pallas_call(
        _nsa_kernel,
        out_shape=jax.ShapeDtypeStruct((b, s, N_HEADS * HEAD_DIM), BF16),
        grid=(b, KV_GROUPS, s // Q_BLOCK),
        in_specs=[
            pl.BlockSpec((1, 1, hq, Q_BLOCK), lambda i, g, q: (i, g, 0, q)),
            pl.BlockSpec((1, 1, nch, HEAD_DIM), lambda i, g, q: (i, g, 0, 0)),
            pl.BlockSpec((1, 1, HEAD_DIM, nch), lambda i, g, q: (i, g, 0, 0)),
            pl.BlockSpec((1, 1, s, 128), lambda i, g, q: (i, g, 0, 0)),
            pl.BlockSpec((1, 1, 128, s), lambda i, g, q: (i, g, 0, 0)),
            pl.BlockSpec((1, 1, GATE_ROWS, Q_BLOCK), lambda i, g, q: (i, g, 0, q)),
            _const_spec((nsel, nch)),
            pl.BlockSpec((1, 4, KEY_TILE, QROWS), lambda i, g, q: (g, 0, 0, 0)),
            pl.BlockSpec((1, 1, QROWS), lambda i, g, q: (g, 0, 0)),
        ],
        out_specs=pl.BlockSpec((1, Q_BLOCK, hq), lambda i, g, q: (i, q, g)),
        scratch_shapes=[pltpu.VMEM((nsel, Q_BLOCK), F32),
                        pltpu.SMEM((max(nsel // WORD_BLOCKS, 1),), jnp.int32),
                        pltpu.SMEM((s // KEY_TILE + SEL_UNROLL,), jnp.int32)],
        compiler_params=_cparams("parallel", "parallel", "arbitrary"),
        name="nsa_attention",
    )(qt, kc, vct, kk, vt, gt, ovt, tabs, slope_rows)


OUT_ROWS = 512


def _outproj_kernel(h_ref, a_ref, o_ref, wa_ref, wo_ref, g_ref, b_ref, out_ref):
    m = (jnp.dot(a_ref[...], wa_ref[...], preferred_element_type=F32)
         + jnp.dot(o_ref[...], wo_ref[...], preferred_element_type=F32))
    out_ref[...] = _layer_norm(DN_ALPHA * h_ref[...] + m, g_ref[...], b_ref[...])


def _outproj_ln(h, a, o, w_out, g, b):
    n = h.shape[0]
    wa = w_out[:CONV_DIM].astype(BF16)
    wo = w_out[CONV_DIM:].astype(BF16)
    row = lambda width: pl.BlockSpec((OUT_ROWS, width), lambda i: (i, 0))
    return pl.pallas_call(
        _outproj_kernel,
        out_shape=jax.ShapeDtypeStruct((n, D_MODEL), F32),
        grid=(n // OUT_ROWS,),
        in_specs=[row(D_MODEL), row(CONV_DIM), row(QCOLS),
                  _const_spec((CONV_DIM, D_MODEL)), _const_spec((QCOLS, D_MODEL)),
                  _const_spec((1, D_MODEL)), _const_spec((1, D_MODEL))],
        out_specs=row(D_MODEL),
        compiler_params=_cparams("parallel"),
        name="mixer_out_ln",
    )(h, a, o, wa, wo, g.reshape(1, -1), b.reshape(1, -1))


S5_ROWS = 256
S5_FOLD = S5_CHUNK * LANES
S5_HALF = S5_SUPER * S5_STATE
SCAN_COLS = 512
S5_OUT_COLS = 256


def _s5_operators(a_re, a_im, log_dt, b_re, b_im, c_re, c_im):
    hi = lax.Precision.HIGHEST
    dt = jnp.exp(log_dt)[:, None]
    ar, ai = a_re, a_im
    mag = jnp.exp(ar * dt)
    lr, li = mag * jnp.cos(ai * dt), mag * jnp.sin(ai * dt)
    den = ar * ar + ai * ai
    zr = ((lr - 1.0) * ar + li * ai) / den
    zi = (li * ar - (lr - 1.0) * ai) / den
    bbr = zr[..., None] * b_re - zi[..., None] * b_im
    bbi = zr[..., None] * b_im + zi[..., None] * b_re
    tau = jnp.arange(S5_CHUNK + 1, dtype=F32)[:, None, None]
    pmag = jnp.exp(ar * dt * tau)
    pr, pi = pmag * jnp.cos(ai * dt * tau), pmag * jnp.sin(ai * dt * tau)
    clr = c_re[None] * pr[:, :, None, :] - c_im[None] * pi[:, :, None, :]
    cli = c_re[None] * pi[:, :, None, :] + c_im[None] * pr[:, :, None, :]
    kern = (jnp.einsum('tgxp,gpc->gtxc', clr[:S5_CHUNK], bbr, precision=hi)
            - jnp.einsum('tgxp,gpc->gtxc', cli[:S5_CHUNK], bbi, precision=hi))
    eye = jnp.eye(S5_SUPER, dtype=F32)
    k8 = kern.reshape(N_SUPER, S5_SUPER, S5_CHUNK, S5_GROUP, S5_GROUP)
    bd = (k8.transpose(0, 2, 1, 4, 3)[:, :, :, :, None, :] * eye[None, None, :, None, :, None])
    bd = bd.reshape(N_SUPER, S5_CHUNK, LANES, LANES)
    rev = S5_CHUNK - 1 - jnp.arange(S5_CHUNK)
    ppr = pr[rev][:, :, :, None] * bbr[None] - pi[rev][:, :, :, None] * bbi[None]
    ppi = pr[rev][:, :, :, None] * bbi[None] + pi[rev][:, :, :, None] * bbr[None]
    p8 = jnp.stack([ppr, ppi], axis=0).reshape(2, S5_CHUNK, N_SUPER, S5_SUPER, S5_STATE, S5_GROUP)
    p_op = p8.transpose(2, 1, 3, 5, 0, 4).reshape(N_SUPER, S5_FOLD, 2 * S5_STATE)
    q8 = jnp.stack([clr[1:], -cli[1:]], axis=0).reshape(2, S5_CHUNK, N_SUPER, S5_SUPER, S5_GROUP, S5_STATE)
    q_op = q8.transpose(2, 0, 3, 5, 1, 4).reshape(N_SUPER, 2 * S5_HALF, S5_CHUNK * S5_GROUP)
    dr = pr[S5_CHUNK].reshape(1, -1)
    di = pi[S5_CHUNK].reshape(1, -1)
    return bd.astype(BF16), p_op.astype(BF16), q_op.astype(BF16), dr, di


def _fold_rows(x_ref):
    chunks = x_ref.shape[0] // S5_CHUNK
    return [x_ref[pl.ds(l, chunks, stride=S5_CHUNK), :] for l in range(S5_CHUNK)]


EXPAND_ROWS = 256


def _expand_block_diag(dst_ref, src_ref, rep_ref, row_shift, col_shift):
    n_rows, n_cols = dst_ref.shape
    for r0 in range(0, n_rows, EXPAND_ROWS):
        full = jnp.dot(src_ref[0, r0:r0 + EXPAND_ROWS, :], rep_ref[...], preferred_element_type=F32)
        row_g = ((lax.broadcasted_iota(jnp.int32, full.shape, 0) + r0) >> row_shift) & (S5_SUPER - 1)
        col_g = (lax.broadcasted_iota(jnp.int32, full.shape, 1) >> col_shift) & (S5_SUPER - 1)
        dst_ref[r0:r0 + EXPAND_ROWS, :] = jnp.where(row_g == col_g, full, 0.0).astype(dst_ref.dtype)


def _s5_state_kernel(x_ref, p_ref, rep_ref, vr_ref, vi_ref, pfull_ref):
    @pl.when(pl.program_id(1) == 0)
    def _():
        _expand_block_diag(pfull_ref, p_ref, rep_ref, S5_GROUP.bit_length() - 1, S5_STATE.bit_length() - 1)

    x2 = jnp.concatenate(_fold_rows(x_ref), axis=1).astype(BF16)
    v = jnp.dot(x2, pfull_ref[...], preferred_element_type=F32)
    vr_ref[...] = v[:, :S5_HALF]
    vi_ref[...] = v[:, S5_HALF:]


def _s5_scan_kernel(vr_ref, vi_ref, dr_ref, di_ref, hr_ref, hi_ref):
    dr = dr_ref[...]
    di = di_ref[...]

    def body(k, carry):
        hr, hi = carry
        hr_ref[0, pl.ds(k, 1), :] = hr
        hi_ref[0, pl.ds(k, 1), :] = hi
        vr = vr_ref[0, pl.ds(k, 1), :]
        vi = vi_ref[0, pl.ds(k, 1), :]
        return dr * hr - di * hi + vr, dr * hi + di * hr + vi

    zero = jnp.zeros((1, SCAN_COLS), F32)
    lax.fori_loop(0, vr_ref.shape[1], body, (zero, zero))


def _s5_out_kernel(x_ref, hr_ref, hi_ref, bd_ref, q_ref, rep_ref, d_ref, y_ref, m_ref, qfull_ref):
    @pl.when(pl.program_id(1) == 0)
    def _():
        _expand_block_diag(qfull_ref, q_ref, rep_ref, S5_STATE.bit_length() - 1, S5_GROUP.bit_length() - 1)
        zero = jnp.zeros((LANES, LANES), m_ref.dtype)
        for l_in in range(S5_CHUNK):
            for l_out in range(S5_CHUNK):
                blk = bd_ref[0, l_out - l_in] if l_out >= l_in else zero
                m_ref[l_in * LANES:(l_in + 1) * LANES, l_out * LANES:(l_out + 1) * LANES] = blk

    xs = _fold_rows(x_ref)
    x2 = jnp.concatenate(xs, axis=1).astype(BF16)
    hcat = jnp.concatenate([hr_ref[...], hi_ref[...]], axis=1).astype(BF16)
    d = d_ref[...]
    chunks = y_ref.shape[0] // S5_CHUNK
    per = S5_OUT_COLS // LANES
    for j in range(S5_FOLD // S5_OUT_COLS):
        cols = slice(j * S5_OUT_COLS, (j + 1) * S5_OUT_COLS)
        kdim = (j + 1) * S5_OUT_COLS
        y = (jnp.dot(x2[:, :kdim], m_ref[:kdim, cols], preferred_element_type=F32)
             + jnp.dot(hcat, qfull_ref[:, cols], preferred_element_type=F32))
        for i in range(per):
            l = j * per + i
            y_ref[pl.ds(l, chunks, stride=S5_CHUNK), :] = jax.nn.gelu(y[:, i * LANES:(i + 1) * LANES] + d * xs[l])


def _glu_ln_kernel(h_ref, y_ref, w_ref, g_ref, b_ref, o_ref):
    r = jnp.dot(y_ref[...].astype(BF16), w_ref[...], preferred_element_type=F32)
    m = r[:, :D_MODEL] * jax.nn.sigmoid(r[:, D_MODEL:])
    o_ref[...] = _layer_norm(DN_ALPHA * h_ref[...] + m, g_ref[...], b_ref[...])


def _s5_mixer_ln(h, nbatch, params, d_skip, w_glu, g, b):
    n = h.shape[0]
    nrow = n // S5_CHUNK
    rows = min(S5_ROWS, nrow)
    bd_op, p_op, q_op, dr, di = _s5_operators(*params)
    xspec = pl.BlockSpec((rows * S5_CHUNK, LANES), lambda s, i: (i, s))
    half = pl.BlockSpec((rows, S5_HALF), lambda s, i: (i, s))
    states = N_SUPER * S5_HALF
    rep_p = np.zeros((2, S5_STATE, 2, S5_SUPER, S5_STATE), np.float32)
    rep_p[np.arange(2)[:, None], np.arange(S5_STATE)[None, :], np.arange(2)[:, None], :, np.arange(S5_STATE)[None, :]] = 1.0
    rep_p = jnp.asarray(rep_p.reshape(2 * S5_STATE, 2 * S5_HALF), BF16)
    rep_q = np.zeros((S5_CHUNK, S5_GROUP, S5_CHUNK, S5_SUPER, S5_GROUP), np.float32)
    rep_q[np.arange(S5_CHUNK)[:, None], np.arange(S5_GROUP)[None, :], np.arange(S5_CHUNK)[:, None], :, np.arange(S5_GROUP)[None, :]] = 1.0
    rep_q = jnp.asarray(rep_q.reshape(S5_CHUNK * S5_GROUP, S5_FOLD), BF16)
    vr, vi = pl.pallas_call(
        _s5_state_kernel,
        out_shape=[jax.ShapeDtypeStruct((nrow, states), F32)] * 2,
        grid=(N_SUPER, nrow // rows),
        in_specs=[xspec, pl.BlockSpec((1, S5_FOLD, 2 * S5_STATE), lambda s, i: (s, 0, 0)),
                  _const_spec((2 * S5_STATE, 2 * S5_HALF))],
        out_specs=[half, half],
        scratch_shapes=[pltpu.VMEM((S5_FOLD, 2 * S5_HALF), BF16)],
        compiler_params=_cparams("arbitrary", "arbitrary"),
        name="s5_state_in",
    )(h, p_op, rep_p)
    nch = nrow // nbatch
    seq = pl.BlockSpec((1, nch, SCAN_COLS), lambda i, c: (i, 0, c))
    dspec = pl.BlockSpec((1, SCAN_COLS), lambda i, c: (0, c))
    hr, hi = pl.pallas_call(
        _s5_scan_kernel,
        out_shape=[jax.ShapeDtypeStruct((nbatch, nch, states), F32)] * 2,
        grid=(nbatch, states // SCAN_COLS),
        in_specs=[seq, seq, dspec, dspec],
        out_specs=[seq, seq],
        compiler_params=_cparams("parallel", "parallel"),
        name="s5_scan",
    )(vr.reshape(nbatch, nch, states), vi.reshape(nbatch, nch, states), dr, di)
    y = pl.pallas_call(
        _s5_out_kernel,
        out_shape=jax.ShapeDtypeStruct((n, D_MODEL), F32),
        grid=(N_SUPER, nrow // rows),
        in_specs=[xspec, half, half,
                  pl.BlockSpec((1, S5_CHUNK, LANES, LANES), lambda s, i: (s, 0, 0, 0)),
                  pl.BlockSpec((1, 2 * S5_HALF, S5_CHUNK * S5_GROUP), lambda s, i: (s, 0, 0)),
                  _const_spec((S5_CHUNK * S5_GROUP, S5_FOLD)),
                  pl.BlockSpec((1, LANES), lambda s, i: (0, s))],
        out_specs=xspec,
        scratch_shapes=[pltpu.VMEM((S5_FOLD, S5_FOLD), BF16), pltpu.VMEM((2 * S5_HALF, S5_FOLD), BF16)],
        compiler_params=_cparams("arbitrary", "arbitrary"),
        name="s5_out",
    )(h, hr.reshape(nrow, states), hi.reshape(nrow, states), bd_op, q_op, rep_q, d_skip.reshape(1, -1))
    row = lambda width: pl.BlockSpec((OUT_ROWS, width), lambda i: (i, 0))
    return pl.pallas_call(
        _glu_ln_kernel,
        out_shape=jax.ShapeDtypeStruct((n, D_MODEL), F32),
        grid=(n // OUT_ROWS,),
        in_specs=[row(D_MODEL), row(D_MODEL), _const_spec((D_MODEL, 2 * D_MODEL)),
                  _const_spec((1, D_MODEL)), _const_spec((1, D_MODEL))],
        out_specs=row(D_MODEL),
        compiler_params=_cparams("parallel"),
        name="s5_glu_ln",
    )(h, y, w_glu.astype(BF16), g.reshape(1, -1), b.reshape(1, -1))


def kernel(x, ffn1_w_in, ffn1_w_out, ffn2_w_in, ffn2_w_out, ln_g, ln_b, ev_w_in, ev_conv_w, ev_conv_b, ev_cln_g, ev_cln_b, ev_pe_k, ev_w1_k, ev_w2_k, ev_pe_v, ev_w1_v, ev_w2_v, ev_w_out, od_a_re, od_a_im, od_log_dt, od_b_re, od_b_im, od_c_re, od_c_im, od_d, od_w_glu):
    bsz, seq, dm = x.shape
    assert dm == D_MODEL and seq % PROJ_ROWS == 0 and seq >= 2 * WINDOW
    n = bsz * seq
    h = x.reshape(n, dm)
    for layer in range(DEPTH):
        i = layer // 2
        h = _ffn_ln(h, ffn1_w_in[layer], ffn1_w_out[layer], ln_g[layer, 0], ln_b[layer, 0])
        if layer % 2 == 0:
            a, kcr, vcr, kk, qt, vt, gt = _even_proj(h.reshape(bsz, seq, dm), ev_w_in[i])
            a = _conv_module(a, ev_conv_w[i], ev_conv_b[i], ev_cln_g[i], ev_cln_b[i])
            kc, vct = _compress(kcr, vcr, ev_pe_k[i], ev_w1_k[i], ev_w2_k[i],
                                ev_pe_v[i], ev_w1_v[i], ev_w2_v[i])
            o = _nsa(qt, kc, vct, kk, vt, gt)
            h = _outproj_ln(h, a.reshape(n, CONV_DIM), o.reshape(n, QCOLS), ev_w_out[i],
                            ln_g[layer, 1], ln_b[layer, 1])
        else:
            params = (od_a_re[i], od_a_im[i], od_log_dt[i], od_b_re[i], od_b_im[i],
                      od_c_re[i], od_c_im[i])
            h = _s5_mixer_ln(h, bsz, params, od_d[i], od_w_glu[i], ln_g[layer, 1], ln_b[layer, 1])
        h = _ffn_ln(h, ffn2_w_in[layer], ffn2_w_out[layer], ln_g[layer, 2], ln_b[layer, 2])
    return h.reshape(bsz, seq, dm)
```

```python
import jax
import jax.numpy as jnp
import numpy as np
from jax import lax
from jax.experimental import pallas as pl
from jax.experimental.pallas import tpu as pltpu

F32 = jnp.float32
BF16 = jnp.bfloat16

D_MODEL = 1024
DEPTH = 4
FFN_DIM = ((8 * D_MODEL // 3 + 127) // 128) * 128
DN_ALPHA = (2.0 * DEPTH) ** 0.25
LN_EPS = 1e-5
CONV_DIM = D_MODEL // 2
CONV_WIDTH = 31
HEAD_DIM = 64
N_HEADS = (D_MODEL // 2) // HEAD_DIM
KV_GROUPS = max(1, N_HEADS // 4)
HPG = N_HEADS // KV_GROUPS
CMP_LEN = 32
CMP_STRIDE = 16
CMP_HIDDEN = 128
SEL_BLOCK = 64
SEL_TOPN = 16
WINDOW = 512
Q_BLOCK = 128
FORCE_SCORE = 1e4
S5_GROUP = 16
S5_GROUPS = D_MODEL // S5_GROUP
S5_STATE = 64
S5_CHUNK = 16
S5_SUPER = 8
N_SUPER = S5_GROUPS // S5_SUPER

LANES = 128
SUBLANES = 8
VMEM_LIMIT_BYTES = 56 * 1024 * 1024

NEG_BIG = -3e38
MASK_NEG = -2.0 ** 100
FEAT_COLS = 8
NULL_COL = HEAD_DIM + 4
KEY_TILE = 128
QROWS = HPG * Q_BLOCK
WIN_TILES = WINDOW // KEY_TILE + 1
SEL_SHIFT = SEL_BLOCK.bit_length() - 1
TILE_BLOCKS = KEY_TILE // SEL_BLOCK
WORD_BLOCKS = 16
SEL_UNROLL = 12


def _cparams(*sem):
    return pltpu.CompilerParams(dimension_semantics=sem, vmem_limit_bytes=VMEM_LIMIT_BYTES)


def _const_spec(shape):
    nd = len(shape)
    return pl.BlockSpec(shape, lambda *_: (0,) * nd)


def _layer_norm(y, g, b):
    mu = jnp.mean(y, axis=-1, keepdims=True)
    yc = y - mu
    var = jnp.mean(yc * yc, axis=-1, keepdims=True)
    return yc * lax.rsqrt(var + LN_EPS) * g + b


FFN_ROWS = 512
FFN_CHUNK = 256


def _ffn_kernel(x_ref, win_ref, wout_ref, g_ref, b_ref, o_ref):
    x = x_ref[...]
    xb = x.astype(BF16)
    acc = None
    for c in range(FFN_DIM // FFN_CHUNK):
        lo = c * FFN_CHUNK
        gate = jnp.dot(xb, win_ref[:, lo:lo + FFN_CHUNK], preferred_element_type=F32)
        val = jnp.dot(xb, win_ref[:, FFN_DIM + lo:FFN_DIM + lo + FFN_CHUNK],
                      preferred_element_type=F32)
        act = (gate * jax.nn.sigmoid(gate) * val).astype(BF16)
        part = jnp.dot(act, wout_ref[lo:lo + FFN_CHUNK, :], preferred_element_type=F32)
        acc = part if acc is None else acc + part
    o_ref[...] = _layer_norm(DN_ALPHA * x + 0.5 * acc, g_ref[...], b_ref[...])


def _ffn_ln(h, w_in, w_out, g, b):
    n = h.shape[0]
    return pl.pallas_call(
        _ffn_kernel,
        out_shape=jax.ShapeDtypeStruct((n, D_MODEL), F32),
        grid=(n // FFN_ROWS,),
        in_specs=[
            pl.BlockSpec((FFN_ROWS, D_MODEL), lambda i: (i, 0)),
            _const_spec((D_MODEL, 2 * FFN_DIM)),
            _const_spec((FFN_DIM, D_MODEL)),
            _const_spec((1, D_MODEL)),
            _const_spec((1, D_MODEL)),
        ],
        out_specs=pl.BlockSpec((FFN_ROWS, D_MODEL), lambda i: (i, 0)),
        compiler_params=_cparams("parallel"),
        name="ffn_ln",
    )(h, w_in.astype(BF16), w_out.astype(BF16), g.reshape(1, -1), b.reshape(1, -1))


PROJ_ROWS = 512
GD = KV_GROUPS * HEAD_DIM
QCOLS = N_HEADS * HEAD_DIM
SEL_K = 2 * LANES
WIN_K = LANES
NAT_COLS = 2 * CONV_DIM + 2 * GD + KV_GROUPS * (SEL_K + WIN_K)
GATE_ROWS = 16
TR_ROWS = QCOLS + KV_GROUPS * 2 * HEAD_DIM + KV_GROUPS * GATE_ROWS


def _evproj_kernel(x_ref, wn_ref, wt_ref, kcs_ref, kcw_ref,
                   a_ref, kcr_ref, vcr_ref, ksel_ref, kwin_ref, qt_ref, vt_ref, gt_ref):
    xb = x_ref[0].astype(BF16)
    r = jnp.dot(xb, wn_ref[...], preferred_element_type=F32)
    a_ref[0] = r[:, :CONV_DIM] * jax.nn.sigmoid(r[:, CONV_DIM:2 * CONV_DIM])
    off = 2 * CONV_DIM
    kcr_ref[0] = r[:, off:off + GD]
    vcr_ref[0] = r[:, off + GD:off + 2 * GD]
    off += 2 * GD
    for g in range(KV_GROUPS):
        lo = off + g * (SEL_K + WIN_K)
        ksel_ref[0, g] = (r[:, lo:lo + SEL_K] + kcs_ref[...]).astype(BF16)
        kwin_ref[0, g] = (r[:, lo + SEL_K:lo + SEL_K + WIN_K] + kcw_ref[...]).astype(BF16)
    rt = lax.dot_general(wt_ref[...], xb, (((1,), (1,)), ((), ())), preferred_element_type=F32)
    hq = HPG * HEAD_DIM
    for g in range(KV_GROUPS):
        qt_ref[0, g] = (rt[g * hq:(g + 1) * hq] * (HEAD_DIM ** -0.5)).astype(BF16)
        lo = QCOLS + g * 128
        vt_ref[0, g] = rt[lo:lo + 128].astype(BF16)
        lo = QCOLS + KV_GROUPS * 128 + g * GATE_ROWS
        gt_ref[0, g] = jax.nn.sigmoid(rt[lo:lo + GATE_ROWS])


def _even_proj_weights(w_in):
    gd = GD
    o = 2 * CONV_DIM
    a_in = w_in[:, :o]
    q = w_in[:, o:o + QCOLS]
    o += QCOLS
    kc, vc, ks, vs, kw, vw = [w_in[:, o + i * gd:o + (i + 1) * gd] for i in range(6)]
    gates = w_in[:, o + 6 * gd:]
    nat = [a_in, kc, vc]
    tr = [q]
    for g in range(KV_GROUPS):
        sl = slice(g * HEAD_DIM, (g + 1) * HEAD_DIM)
        nat += [jnp.pad(ks[:, sl], ((0, 0), (0, SEL_K - HEAD_DIM))),
                jnp.pad(kw[:, sl], ((0, 0), (0, WIN_K - HEAD_DIM)))]
    for g in range(KV_GROUPS):
        sl = slice(g * HEAD_DIM, (g + 1) * HEAD_DIM)
        tr += [vs[:, sl], vw[:, sl]]
    for g in range(KV_GROUPS):
        gg = gates[:, g * 3 * HPG:(g + 1) * 3 * HPG]
        tr.append(jnp.pad(gg, ((0, 0), (0, GATE_ROWS - 3 * HPG))))
    wn = jnp.concatenate(nat, axis=1).astype(BF16)
    wt = jnp.concatenate(tr, axis=1).T.astype(BF16)
    return wn, wt


def _position_features(pos):
    feat = np.zeros((pos.shape[0], WIN_K), np.float32)
    feat[:, HEAD_DIM] = pos >> SEL_SHIFT
    feat[:, HEAD_DIM + 1] = pos & (SEL_BLOCK - 1)
    feat[:, HEAD_DIM + 2:HEAD_DIM + 4] = 1.0
    return feat


def _key_constants(s):
    pos = np.arange(s)
    feat = _position_features(pos)
    onehot = (pos[:, None] >> SEL_SHIFT == np.arange(SEL_K - WIN_K)[None, :]).astype(np.float32)
    return jnp.asarray(np.concatenate([feat, onehot], axis=1)), jnp.asarray(feat)


def _even_proj(h3, w_in):
    b, s, _ = h3.shape
    wn, wt = _even_proj_weights(w_in)
    kcs, kcw = _key_constants(s)
    nt = s // PROJ_ROWS
    rows = PROJ_ROWS
    return pl.pallas_call(
        _evproj_kernel,
        out_shape=[
            jax.ShapeDtypeStruct((b, s, CONV_DIM), F32),
            jax.ShapeDtypeStruct((b, s, GD), F32),
            jax.ShapeDtypeStruct((b, s, GD), F32),
            jax.ShapeDtypeStruct((b, KV_GROUPS, s, SEL_K), BF16),
            jax.ShapeDtypeStruct((b, KV_GROUPS, s, WIN_K), BF16),
            jax.ShapeDtypeStruct((b, KV_GROUPS, HPG * HEAD_DIM, s), BF16),
            jax.ShapeDtypeStruct((b, KV_GROUPS, 128, s), BF16),
            jax.ShapeDtypeStruct((b, KV_GROUPS, GATE_ROWS, s), F32),
        ],
        grid=(b, nt),
        in_specs=[
            pl.BlockSpec((1, rows, D_MODEL), lambda i, j: (i, j, 0)),
            _const_spec((D_MODEL, NAT_COLS)),
            _const_spec((TR_ROWS, D_MODEL)),
            pl.BlockSpec((rows, SEL_K), lambda i, j: (j, 0)),
            pl.BlockSpec((rows, WIN_K), lambda i, j: (j, 0)),
        ],
        out_specs=[
            pl.BlockSpec((1, rows, CONV_DIM), lambda i, j: (i, j, 0)),
            pl.BlockSpec((1, rows, GD), lambda i, j: (i, j, 0)),
            pl.BlockSpec((1, rows, GD), lambda i, j: (i, j, 0)),
            pl.BlockSpec((1, KV_GROUPS, rows, SEL_K), lambda i, j: (i, 0, j, 0)),
            pl.BlockSpec((1, KV_GROUPS, rows, WIN_K), lambda i, j: (i, 0, j, 0)),
            pl.BlockSpec((1, KV_GROUPS, HPG * HEAD_DIM, rows), lambda i, j: (i, 0, 0, j)),
            pl.BlockSpec((1, KV_GROUPS, 128, rows), lambda i, j: (i, 0, 0, j)),
            pl.BlockSpec((1, KV_GROUPS, GATE_ROWS, rows), lambda i, j: (i, 0, 0, j)),
        ],
        compiler_params=_cparams("parallel", "parallel"),
        name="even_proj",
    )(h3, wn, wt, kcs, kcw)


CONV_ROWS = 256
CONV_HALO = 32
CONV_SUB = 128


def _conv_kernel(cur_ref, halo_ref, w_ref, cb_ref, g_ref, b_ref, o_ref, buf_ref, acc_ref, sh_ref):
    first = pl.program_id(1) == 0
    buf_ref[0:CONV_HALO, :] = jnp.where(first, 0.0, halo_ref[0])
    buf_ref[CONV_HALO:CONV_HALO + CONV_ROWS, :] = cur_ref[0]
    lead = CONV_HALO - (CONV_WIDTH - 1)
    for c in range(CONV_DIM // LANES):
        cols = slice(c * LANES, (c + 1) * LANES)
        for rc in range(CONV_ROWS // CONV_SUB):
            acc = jnp.zeros((CONV_SUB, LANES), F32) + cb_ref[:, cols]
            for r in range(SUBLANES):
                base = lead + r + rc * CONV_SUB
                taps = range(r, CONV_WIDTH, SUBLANES)
                span = CONV_SUB + (len(taps) - 1) * SUBLANES
                sh_ref[0:span, :] = buf_ref[base:base + span, cols]
                for k in taps:
                    acc = acc + w_ref[k][:, cols] * sh_ref[k - r:k - r + CONV_SUB, :]
            acc_ref[rc * CONV_SUB:(rc + 1) * CONV_SUB, cols] = acc
    y = _layer_norm(acc_ref[...], g_ref[...], b_ref[...])
    o_ref[0] = (y * jax.nn.sigmoid(y)).astype(BF16)


def _conv_module(a, conv_w, conv_b, cln_g, cln_b):
    b, s, _ = a.shape
    per = CONV_ROWS // CONV_HALO
    return pl.pallas_call(
        _conv_kernel,
        out_shape=jax.ShapeDtypeStruct((b, s, CONV_DIM), BF16),
        grid=(b, s // CONV_ROWS),
        in_specs=[
            pl.BlockSpec((1, CONV_ROWS, CONV_DIM), lambda i, j: (i, j, 0)),
            pl.BlockSpec((1, CONV_HALO, CONV_DIM), lambda i, j: (i, jnp.maximum(j * per - 1, 0), 0)),
            _const_spec((CONV_WIDTH, 1, CONV_DIM)),
            _const_spec((1, CONV_DIM)),
            _const_spec((1, CONV_DIM)),
            _const_spec((1, CONV_DIM)),
        ],
        out_specs=pl.BlockSpec((1, CONV_ROWS, CONV_DIM), lambda i, j: (i, j, 0)),
        scratch_shapes=[pltpu.VMEM((CONV_HALO + CONV_ROWS, CONV_DIM), F32),
                        pltpu.VMEM((CONV_ROWS, CONV_DIM), F32),
                        pltpu.VMEM((CONV_SUB + (-(-CONV_WIDTH // SUBLANES) - 1) * SUBLANES, LANES), F32)],
        compiler_params=_cparams("parallel", "arbitrary"),
        name="conv_module",
    )(a, a, conv_w.reshape(CONV_WIDTH, 1, CONV_DIM), conv_b.reshape(1, -1),
      cln_g.reshape(1, -1), cln_b.reshape(1, -1))


def _compress_one(x_ref, pe_ref, w1_ref, w2_ref):
    nch = x_ref.shape[1]
    first = None
    second = None
    for l in range(CMP_STRIDE):
        xl = x_ref[0, :, l, :]
        pa = jnp.dot((xl + pe_ref[l]).astype(BF16), w1_ref[l], preferred_element_type=F32)
        pb = jnp.dot((xl + pe_ref[CMP_STRIDE + l]).astype(BF16), w1_ref[CMP_STRIDE + l],
                     preferred_element_type=F32)
        first = pa if first is None else first + pa
        second = pb if second is None else second + pb
    u = first + pltpu.roll(second, nch - 1, 0)
    hid = (u * jax.nn.sigmoid(u)).astype(BF16)
    out = jnp.dot(hid, w2_ref[...], preferred_element_type=F32)
    row = lax.broadcasted_iota(jnp.int32, out.shape, 0)
    return jnp.where(row < nch - 1, out, 0.0)


def _compress_kernel(xk_ref, xv_ref, pek_ref, w1k_ref, w2k_ref, pev_ref, w1v_ref, w2v_ref, feat_ref,
                     kc_ref, vct_ref):
    kc = _compress_one(xk_ref, pek_ref, w1k_ref, w2k_ref)
    vct = _compress_one(xv_ref, pev_ref, w1v_ref, w2v_ref).T
    lane = lax.broadcasted_iota(jnp.int32, kc.shape, 1)
    for g in range(KV_GROUPS):
        keys = kc if g == 0 else pltpu.roll(kc, GD - g * HEAD_DIM, 1)
        kc_ref[0, g] = jnp.where(lane < HEAD_DIM, keys, feat_ref[...]).astype(BF16)
        vct_ref[0, g] = vct[g * HEAD_DIM:(g + 1) * HEAD_DIM, :].astype(BF16)


def _compress_weights(pe, w1, w2):
    eye = jnp.eye(KV_GROUPS, dtype=F32)
    pe2 = jnp.tile(pe, (1, KV_GROUPS)).reshape(CMP_LEN, 1, GD)
    w1r = w1.reshape(CMP_LEN, HEAD_DIM, CMP_HIDDEN)
    w1bd = jnp.einsum('ldh,gk->lgdkh', w1r, eye).reshape(CMP_LEN, GD, KV_GROUPS * CMP_HIDDEN)
    w2bd = jnp.einsum('hd,gk->ghkd', w2, eye).reshape(KV_GROUPS * CMP_HIDDEN, GD)
    return pe2, w1bd.astype(BF16), w2bd.astype(BF16)


def _compress(kcr, vcr, pe_k, w1_k, w2_k, pe_v, w1_v, w2_v):
    b, s, _ = kcr.shape
    nch = s // CMP_STRIDE
    xk = kcr.reshape(b, nch, CMP_STRIDE, GD)
    xv = vcr.reshape(b, nch, CMP_STRIDE, GD)
    wk = _compress_weights(pe_k, w1_k, w2_k)
    wv = _compress_weights(pe_v, w1_v, w2_v)
    xspec = pl.BlockSpec((1, nch, CMP_STRIDE, GD), lambda i: (i, 0, 0, 0))
    wspecs = [_const_spec((CMP_LEN, 1, GD)), _const_spec((CMP_LEN, GD, KV_GROUPS * CMP_HIDDEN)),
              _const_spec((KV_GROUPS * CMP_HIDDEN, GD))]
    assert GD == WIN_K
    feat = jnp.asarray(_position_features(np.arange(nch) * CMP_STRIDE + CMP_LEN - 1))
    return pl.pallas_call(
        _compress_kernel,
        out_shape=[jax.ShapeDtypeStruct((b, KV_GROUPS, nch, WIN_K), BF16),
                   jax.ShapeDtypeStruct((b, KV_GROUPS, HEAD_DIM, nch), BF16)],
        grid=(b,),
        in_specs=[xspec, xspec] + wspecs + wspecs + [_const_spec((nch, WIN_K))],
        out_specs=[pl.BlockSpec((1, KV_GROUPS, nch, WIN_K), lambda i: (i, 0, 0, 0)),
                   pl.BlockSpec((1, KV_GROUPS, HEAD_DIM, nch), lambda i: (i, 0, 0, 0))],
        compiler_params=_cparams("parallel"),
        name="compress",
    )(xk, xv, *wk, *wv, feat)


def _attend_step(carry, tiles, q_rhs):
    m, l, acc = carry
    scores = []
    for k, _, mask in tiles:
        s = jnp.dot(k, q_rhs, preferred_element_type=F32)
        scores.append(s if mask is None else s + mask)
    m_new = m
    for s in scores:
        m_new = jnp.maximum(m_new, jnp.max(s, axis=0, keepdims=True))
    probs = [jnp.exp(s - m_new) for s in scores]
    alpha = jnp.exp(m - m_new)
    l = alpha * l
    for p in probs:
        l = l + jnp.sum(p, axis=0, keepdims=True)
    p_all = jnp.concatenate([p.astype(BF16) for p in probs], axis=0)
    v_all = jnp.concatenate([v for _, v, _ in tiles], axis=1)
    acc = alpha * acc + jnp.dot(v_all, p_all, preferred_element_type=F32)
    return m_new, l, acc


def _nsa_kernel(qt_ref, kc_ref, vct_ref, ksel_ref, kwin_ref, vt_ref, gt_ref, ovt_ref, tab_ref,
                knull_ref, slope_ref, o_ref, qrhs_ref, words_ref, list_ref):
    qb = pl.program_id(2)
    t0 = qb * Q_BLOCK
    nsel = ovt_ref.shape[0]
    qt = qt_ref[0, 0]
    qcat = jnp.concatenate([qt[j * HEAD_DIM:(j + 1) * HEAD_DIM, :] for j in range(HPG)], axis=1)
    slope = slope_ref[0]

    qpos_row = t0 + (lax.broadcasted_iota(jnp.int32, (1, QROWS), 1) & (Q_BLOCK - 1))
    q_hi = (qpos_row >> SEL_SHIFT).astype(F32)
    q_lo = (qpos_row & (SEL_BLOCK - 1)).astype(F32)
    feat = jnp.concatenate([slope * SEL_BLOCK, slope, -(slope * SEL_BLOCK) * q_hi, -slope * q_lo,
                            jnp.full((1, QROWS), MASK_NEG, F32),
                            jnp.zeros((FEAT_COLS - 5, QROWS), F32)], axis=0)
    q_win = jnp.concatenate([qcat.astype(F32), feat,
                             jnp.zeros((WIN_K - HEAD_DIM - FEAT_COLS, QROWS), F32)], axis=0).astype(BF16)
    qrhs_ref[0:WIN_K, :] = q_win

    nc = kc_ref.shape[2]
    s = jnp.dot(kc_ref[0, 0], q_win, preferred_element_type=F32)
    cend = lax.broadcasted_iota(jnp.int32, (nc, QROWS), 0) * CMP_STRIDE + (CMP_LEN - 1)
    qpos = t0 + (lax.broadcasted_iota(jnp.int32, (nc, QROWS), 1) & (Q_BLOCK - 1))
    s = jnp.where(qpos >= cend, s, -jnp.inf)
    m_c = jnp.maximum(jnp.max(s, axis=0, keepdims=True), NEG_BIG)
    p = jnp.exp(s - m_c)
    l_c = jnp.sum(p, axis=0, keepdims=True)
    pb = p.astype(BF16)
    inv_c = jnp.where(l_c > 0, 1.0 / l_c, 0.0)
    o_cmp = jnp.dot(vct_ref[0, 0], pb, preferred_element_type=F32) * inv_c
    imp4 = jnp.dot(ovt_ref[...], pb, preferred_element_type=F32) * inv_c
    imp = imp4[:, 0:Q_BLOCK]
    for j in range(1, HPG):
        imp = imp + imp4[:, j * Q_BLOCK:(j + 1) * Q_BLOCK]

    init = (jnp.full((1, QROWS), NEG_BIG, F32), jnp.zeros((1, QROWS), F32),
            jnp.zeros((HEAD_DIM, QROWS), F32))

    def key_tile(k_ref, t_raw, v_lo, mask=None, maybe_null=True):
        k0 = pl.multiple_of(jnp.maximum(t_raw, 0) * KEY_TILE, KEY_TILE)
        keys = k_ref[0, 0, pl.ds(k0, KEY_TILE), :]
        if maybe_null:
            keys = jnp.where(t_raw < 0, knull_ref[:, 0:keys.shape[1]], keys)
        return keys, vt_ref[0, 0, v_lo:v_lo + HEAD_DIM, pl.ds(k0, KEY_TILE)], mask

    tiles = []
    for w in range(WIN_TILES):
        edge = tab_ref[0] if w == 0 else (tab_ref[1] if w == WIN_TILES - 1 else None)
        tiles.append(key_tile(kwin_ref, qb - (WIN_TILES - 1) + w, HEAD_DIM, edge, w < WIN_TILES - 1))
    _, l_w, acc_w = _attend_step(init, tiles, q_win)

    n_iota = lax.broadcasted_iota(jnp.int32, (nsel, Q_BLOCK), 0)
    cur = (t0 + lax.broadcasted_iota(jnp.int32, (nsel, Q_BLOCK), 1)) >> SEL_SHIFT
    valid = n_iota <= cur
    forced = (n_iota == 0) | (n_iota == cur) | (n_iota == cur - 1)
    score = jnp.where(valid, jnp.where(forced, FORCE_SCORE, imp), -1.0)
    selm = jnp.full((nsel, Q_BLOCK), MASK_NEG, F32)

    def pick(carry):
        score, selm = carry
        best = jnp.max(score, axis=0, keepdims=True)
        first = jnp.min(jnp.where(score == best, n_iota, nsel), axis=0, keepdims=True)
        hit = n_iota == first
        selm = jnp.where(hit & (best >= 0), 0.0, selm)
        return jnp.where(hit, -2.0, score), selm

    carry = (score, selm)
    for _ in range(min(SEL_TOPN, nsel)):
        carry = pick(carry)
    selm = carry[1]
    qrhs_ref[WIN_K:WIN_K + nsel, :] = jnp.concatenate([selm] * HPG, axis=1).astype(BF16)
    if nsel < SEL_K - WIN_K:
        qrhs_ref[WIN_K + nsel:SEL_K, :] = jnp.zeros((SEL_K - WIN_K - nsel, QROWS), BF16)

    active = jnp.max(jnp.where(selm == 0.0, 1.0, 0.0), axis=1, keepdims=True)
    weight = jnp.left_shift(1, lax.broadcasted_iota(jnp.int32, (nsel, 1), 0) & (WORD_BLOCKS - 1))
    packed = active * weight.astype(F32)
    for w in range(nsel // WORD_BLOCKS):
        words_ref[w] = jnp.sum(packed[w * WORD_BLOCKS:(w + 1) * WORD_BLOCKS, :]).astype(jnp.int32)

    tiles_per_word = WORD_BLOCKS // TILE_BLOCKS

    def compact(w, n):
        word = words_ref[w]
        for u in range(tiles_per_word):
            t = w * tiles_per_word + u
            hit = (((word >> (u * TILE_BLOCKS)) & ((1 << TILE_BLOCKS) - 1)) != 0) & (t < qb)
            list_ref[n] = t
            n = n + hit.astype(jnp.int32)
        return n

    n_act = lax.fori_loop(0, (qb >> (tiles_per_word.bit_length() - 1)) + 1, compact, 0)

    def sel_step(i, carry):
        tiles = [key_tile(ksel_ref, list_ref[i * SEL_UNROLL + u], 0, None, False) for u in range(SEL_UNROLL)]
        return _attend_step(carry, tiles, qrhs_ref[...])

    n_full = n_act // SEL_UNROLL
    carry = lax.fori_loop(0, n_full, sel_step, init)
    last = jnp.maximum(n_act - 1, 0)
    tiles = []
    for u in range(SEL_UNROLL - 1):
        idx = n_full * SEL_UNROLL + u
        tiles.append(key_tile(ksel_ref, jnp.where(idx < n_act, list_ref[jnp.minimum(idx, last)], -1), 0))
    tiles.append(key_tile(ksel_ref, qb, 0, tab_ref[1], False))
    _, l_s, acc_s = _attend_step(carry, tiles, qrhs_ref[...])

    gt = gt_ref[0, 0]

    def gate_row(branch):
        return jnp.concatenate([gt[j * 3 + branch:j * 3 + branch + 1, :] for j in range(HPG)], axis=1)

    o_t = gate_row(0) * o_cmp + gate_row(1) * (acc_s / l_s) + gate_row(2) * (acc_w / l_w)
    out = jnp.concatenate([o_t[:, j * Q_BLOCK:(j + 1) * Q_BLOCK].T for j in range(HPG)], axis=1)
    o_ref[0] = out.astype(BF16)


def _nsa_tables(s):
    heads = np.arange(1, N_HEADS + 1, dtype=np.float32)
    slopes = (2.0 ** (-8.0 * heads / N_HEADS)).astype(np.float32).reshape(KV_GROUPS, HPG)
    slope_rows = np.repeat(slopes, Q_BLOCK, axis=1).reshape(KV_GROUPS, 1, QROWS)
    i = np.arange(KEY_TILE)[:, None]
    ql = np.tile(np.arange(Q_BLOCK), HPG)[None, :]
    neg = np.float32(-np.inf)
    tabs = np.stack([np.where(i > ql, np.float32(0), neg), np.where(i <= ql, np.float32(0), neg)], axis=0)
    knull = np.zeros((KEY_TILE, SEL_K), np.float32)
    knull[:, NULL_COL] = 1.0
    nch = s // CMP_STRIDE
    nsel = s // SEL_BLOCK
    c = np.arange(nch)
    n = np.arange(nsel)
    ov = ((c[None, :] * CMP_STRIDE + CMP_LEN - 1 >= n[:, None] * SEL_BLOCK)
          & (c[None, :] * CMP_STRIDE < (n[:, None] + 1) * SEL_BLOCK) & (c[None, :] < nch - 1))
    return jnp.asarray(slope_rows), jnp.asarray(tabs), jnp.asarray(knull, BF16), jnp.asarray(ov, BF16)


def _nsa(qt, kc, vct, ksel, kwin, vt, gt):
    b, _, _, s = qt.shape
    nch = s // CMP_STRIDE
    nsel = s // SEL_BLOCK
    assert nsel <= SEL_K - WIN_K and nsel % WORD_BLOCKS == 0
    slope_rows, tabs, knull, ovt = _nsa_tables(s)
    hq = HPG * HEAD_DIM
    per_group = lambda *shape: pl.BlockSpec((1, 1) + shape, lambda i, g, q: (i, g, 0, 0))
    return pl.pallas_call(
        _nsa_kernel,
        out_shape=jax.ShapeDtypeStruct((b, s, N_HEADS * HEAD_DIM), BF16),
        grid=(b, KV_GROUPS, s // Q_BLOCK),
        in_specs=[
            pl.BlockSpec((1, 1, hq, Q_BLOCK), lambda i, g, q: (i, g, 0, q)),
            per_group(nch, WIN_K),
            per_group(HEAD_DIM, nch),
            per_group(s, SEL_K),
            per_group(s, WIN_K),
            per_group(2 * HEAD_DIM, s),
            pl.BlockSpec((1, 1, GATE_ROWS, Q_BLOCK), lambda i, g, q: (i, g, 0, q)),
            _const_spec((nsel, nch)),
            _const_spec((2, KEY_TILE, QROWS)),
            _const_spec((KEY_TILE, SEL_K)),
            pl.BlockSpec((1, 1, QROWS), lambda i, g, q: (g, 0, 0)),
        ],
        out_specs=pl.BlockSpec((1, Q_BLOCK, hq), lambda i, g, q: (i, q, g)),
        scratch_shapes=[pltpu.VMEM((SEL_K, QROWS), BF16),
                        pltpu.SMEM((nsel // WORD_BLOCKS,), jnp.int32),
                        pltpu.SMEM((s // KEY_TILE + WORD_BLOCKS // TILE_BLOCKS,), jnp.int32)],
        compiler_params=_cparams("parallel", "parallel", "arbitrary"),
        name="nsa_attention",
    )(qt, kc, vct, ksel, kwin, vt, gt, ovt, tabs, knull, slope_rows)


OUT_ROWS = 512


def _outproj_kernel(h_ref, a_ref, o_ref, wa_ref, wo_ref, g_ref, b_ref, out_ref):
    m = (jnp.dot(a_ref[...], wa_ref[...], preferred_element_type=F32)
         + jnp.dot(o_ref[...], wo_ref[...], preferred_element_type=F32))
    out_ref[...] = _layer_norm(DN_ALPHA * h_ref[...] + m, g_ref[...], b_ref[...])


def _outproj_ln(h, a, o, w_out, g, b):
    n = h.shape[0]
    wa = w_out[:CONV_DIM].astype(BF16)
    wo = w_out[CONV_DIM:].astype(BF16)
    row = lambda width: pl.BlockSpec((OUT_ROWS, width), lambda i: (i, 0))
    return pl.pallas_call(
        _outproj_kernel,
        out_shape=jax.ShapeDtypeStruct((n, D_MODEL), F32),
        grid=(n // OUT_ROWS,),
        in_specs=[row(D_MODEL), row(CONV_DIM), row(QCOLS),
                  _const_spec((CONV_DIM, D_MODEL)), _const_spec((QCOLS, D_MODEL)),
                  _const_spec((1, D_MODEL)), _const_spec((1, D_MODEL))],
        out_specs=row(D_MODEL),
        compiler_params=_cparams("parallel"),
        name="mixer_out_ln",
    )(h, a, o, wa, wo, g.reshape(1, -1), b.reshape(1, -1))


S5_ROWS = 256
S5_FOLD = S5_CHUNK * LANES
S5_HALF = S5_SUPER * S5_STATE
SCAN_COLS = 512
S5_OUT_COLS = 256


def _s5_operators(a_re, a_im, log_dt, b_re, b_im, c_re, c_im):
    hi = lax.Precision.HIGHEST
    dt = jnp.exp(log_dt)[:, None]
    ar, ai = a_re, a_im
    mag = jnp.exp(ar * dt)
    lr, li = mag * jnp.cos(ai * dt), mag * jnp.sin(ai * dt)
    den = ar * ar + ai * ai
    zr = ((lr - 1.0) * ar + li * ai) / den
    zi = (li * ar - (lr - 1.0) * ai) / den
    bbr = zr[..., None] * b_re - zi[..., None] * b_im
    bbi = zr[..., None] * b_im + zi[..., None] * b_re
    tau = jnp.arange(S5_CHUNK + 1, dtype=F32)[:, None, None]
    pmag = jnp.exp(ar * dt * tau)
    pr, pi = pmag * jnp.cos(ai * dt * tau), pmag * jnp.sin(ai * dt * tau)
    clr = c_re[None] * pr[:, :, None, :] - c_im[None] * pi[:, :, None, :]
    cli = c_re[None] * pi[:, :, None, :] + c_im[None] * pr[:, :, None, :]
    kern = (jnp.einsum('tgxp,gpc->gtxc', clr[:S5_CHUNK], bbr, precision=hi)
            - jnp.einsum('tgxp,gpc->gtxc', cli[:S5_CHUNK], bbi, precision=hi))
    eye = jnp.eye(S5_SUPER, dtype=F32)
    k8 = kern.reshape(N_SUPER, S5_SUPER, S5_CHUNK, S5_GROUP, S5_GROUP)
    bd = (k8.transpose(0, 2, 1, 4, 3)[:, :, :, :, None, :] * eye[None, None, :, None, :, None])
    bd = bd.reshape(N_SUPER, S5_CHUNK, LANES, LANES)
    rev = S5_CHUNK - 1 - jnp.arange(S5_CHUNK)
    ppr = pr[rev][:, :, :, None] * bbr[None] - pi[rev][:, :, :, None] * bbi[None]
    ppi = pr[rev][:, :, :, None] * bbi[None] + pi[rev][:, :, :, None] * bbr[None]
    p8 = jnp.stack([ppr, ppi], axis=0).reshape(2, S5_CHUNK, N_SUPER, S5_SUPER, S5_STATE, S5_GROUP)
    p_op = p8.transpose(2, 1, 3, 5, 0, 4).reshape(N_SUPER, S5_FOLD, 2 * S5_STATE)
    q8 = jnp.stack([clr[1:], -cli[1:]], axis=0).reshape(2, S5_CHUNK, N_SUPER, S5_SUPER, S5_GROUP, S5_STATE)
    q_op = q8.transpose(2, 0, 3, 5, 1, 4).reshape(N_SUPER, 2 * S5_HALF, S5_CHUNK * S5_GROUP)
    dr = pr[S5_CHUNK].reshape(1, -1)
    di = pi[S5_CHUNK].reshape(1, -1)
    return bd.astype(BF16), p_op.astype(BF16), q_op.astype(BF16), dr, di


def _fold_rows(x_ref):
    chunks = x_ref.shape[0] // S5_CHUNK
    return [x_ref[pl.ds(l, chunks, stride=S5_CHUNK), :] for l in range(S5_CHUNK)]


EXPAND_ROWS = 256


def _expand_block_diag(dst_ref, src_ref, rep_ref, row_shift, col_shift):
    n_rows, n_cols = dst_ref.shape
    for r0 in range(0, n_rows, EXPAND_ROWS):
        full = jnp.dot(src_ref[0, r0:r0 + EXPAND_ROWS, :], rep_ref[...], preferred_element_type=F32)
        row_g = ((lax.broadcasted_iota(jnp.int32, full.shape, 0) + r0) >> row_shift) & (S5_SUPER - 1)
        col_g = (lax.broadcasted_iota(jnp.int32, full.shape, 1) >> col_shift) & (S5_SUPER - 1)
        dst_ref[r0:r0 + EXPAND_ROWS, :] = jnp.where(row_g == col_g, full, 0.0).astype(dst_ref.dtype)


def _s5_state_kernel(x_ref, p_ref, rep_ref, vr_ref, vi_ref, pfull_ref):
    @pl.when(pl.program_id(1) == 0)
    def _():
        _expand_block_diag(pfull_ref, p_ref, rep_ref, S5_GROUP.bit_length() - 1, S5_STATE.bit_length() - 1)

    x2 = jnp.concatenate(_fold_rows(x_ref), axis=1).astype(BF16)
    v = jnp.dot(x2, pfull_ref[...], preferred_element_type=F32)
    vr_ref[...] = v[:, :S5_HALF]
    vi_ref[...] = v[:, S5_HALF:]


def _s5_scan_kernel(vr_ref, vi_ref, dr_ref, di_ref, hr_ref, hi_ref):
    dr = dr_ref[...]
    di = di_ref[...]

    def body(k, carry):
        hr, hi = carry
        hr_ref[0, pl.ds(k, 1), :] = hr
        hi_ref[0, pl.ds(k, 1), :] = hi
        vr = vr_ref[0, pl.ds(k, 1), :]
        vi = vi_ref[0, pl.ds(k, 1), :]
        return dr * hr - di * hi + vr, dr * hi + di * hr + vi

    zero = jnp.zeros((1, SCAN_COLS), F32)
    lax.fori_loop(0, vr_ref.shape[1], body, (zero, zero))


def _s5_out_kernel(x_ref, hr_ref, hi_ref, bd_ref, q_ref, rep_ref, d_ref, y_ref, m_ref, qfull_ref):
    @pl.when(pl.program_id(1) == 0)
    def _():
        _expand_block_diag(qfull_ref, q_ref, rep_ref, S5_STATE.bit_length() - 1, S5_GROUP.bit_length() - 1)
        zero = jnp.zeros((LANES, LANES), m_ref.dtype)
        for l_in in range(S5_CHUNK):
            for l_out in range(S5_CHUNK):
                blk = bd_ref[0, l_out - l_in] if l_out >= l_in else zero
                m_ref[l_in * LANES:(l_in + 1) * LANES, l_out * LANES:(l_out + 1) * LANES] = blk

    xs = _fold_rows(x_ref)
    x2 = jnp.concatenate(xs, axis=1).astype(BF16)
    hcat = jnp.concatenate([hr_ref[...], hi_ref[...]], axis=1).astype(BF16)
    d = d_ref[...]
    chunks = y_ref.shape[0] // S5_CHUNK
    per = S5_OUT_COLS // LANES
    for j in range(S5_FOLD // S5_OUT_COLS):
        cols = slice(j * S5_OUT_COLS, (j + 1) * S5_OUT_COLS)
        kdim = (j + 1) * S5_OUT_COLS
        y = (jnp.dot(x2[:, :kdim], m_ref[:kdim, cols], preferred_element_type=F32)
             + jnp.dot(hcat, qfull_ref[:, cols], preferred_element_type=F32))
        for i in range(per):
            l = j * per + i
            y_ref[pl.ds(l, chunks, stride=S5_CHUNK), :] = jax.nn.gelu(y[:, i * LANES:(i + 1) * LANES] + d * xs[l])


def _glu_ln_kernel(h_ref, y_ref, w_ref, g_ref, b_ref, o_ref):
    r = jnp.dot(y_ref[...].astype(BF16), w_ref[...], preferred_element_type=F32)
    m = r[:, :D_MODEL] * jax.nn.sigmoid(r[:, D_MODEL:])
    o_ref[...] = _layer_norm(DN_ALPHA * h_ref[...] + m, g_ref[...], b_ref[...])


def _s5_mixer_ln(h, nbatch, params, d_skip, w_glu, g, b):
    n = h.shape[0]
    nrow = n // S5_CHUNK
    rows = min(S5_ROWS, nrow)
    bd_op, p_op, q_op, dr, di = _s5_operators(*params)
    xspec = pl.BlockSpec((rows * S5_CHUNK, LANES), lambda s, i: (i, s))
    half = pl.BlockSpec((rows, S5_HALF), lambda s, i: (i, s))
    states = N_SUPER * S5_HALF
    rep_p = np.zeros((2, S5_STATE, 2, S5_SUPER, S5_STATE), np.float32)
    rep_p[np.arange(2)[:, None], np.arange(S5_STATE)[None, :], np.arange(2)[:, None], :, np.arange(S5_STATE)[None, :]] = 1.0
    rep_p = jnp.asarray(rep_p.reshape(2 * S5_STATE, 2 * S5_HALF), BF16)
    rep_q = np.zeros((S5_CHUNK, S5_GROUP, S5_CHUNK, S5_SUPER, S5_GROUP), np.float32)
    rep_q[np.arange(S5_CHUNK)[:, None], np.arange(S5_GROUP)[None, :], np.arange(S5_CHUNK)[:, None], :, np.arange(S5_GROUP)[None, :]] = 1.0
    rep_q = jnp.asarray(rep_q.reshape(S5_CHUNK * S5_GROUP, S5_FOLD), BF16)
    vr, vi = pl.pallas_call(
        _s5_state_kernel,
        out_shape=[jax.ShapeDtypeStruct((nrow, states), F32)] * 2,
        grid=(N_SUPER, nrow // rows),
        in_specs=[xspec, pl.BlockSpec((1, S5_FOLD, 2 * S5_STATE), lambda s, i: (s, 0, 0)),
                  _const_spec((2 * S5_STATE, 2 * S5_HALF))],
        out_specs=[half, half],
        scratch_shapes=[pltpu.VMEM((S5_FOLD, 2 * S5_HALF), BF16)],
        compiler_params=_cparams("arbitrary", "arbitrary"),
        name="s5_state_in",
    )(h, p_op, rep_p)
    nch = nrow // nbatch
    seq = pl.BlockSpec((1, nch, SCAN_COLS), lambda i, c: (i, 0, c))
    dspec = pl.BlockSpec((1, SCAN_COLS), lambda i, c: (0, c))
    hr, hi = pl.pallas_call(
        _s5_scan_kernel,
        out_shape=[jax.ShapeDtypeStruct((nbatch, nch, states), F32)] * 2,
        grid=(nbatch, states // SCAN_COLS),
        in_specs=[seq, seq, dspec, dspec],
        out_specs=[seq, seq],
        compiler_params=_cparams("parallel", "parallel"),
        name="s5_scan",
    )(vr.reshape(nbatch, nch, states), vi.reshape(nbatch, nch, states), dr, di)
    y = pl.pallas_call(
        _s5_out_kernel,
        out_shape=jax.ShapeDtypeStruct((n, D_MODEL), F32),
        grid=(N_SUPER, nrow // rows),
        in_specs=[xspec, half, half,
                  pl.BlockSpec((1, S5_CHUNK, LANES, LANES), lambda s, i: (s, 0, 0, 0)),
                  pl.BlockSpec((1, 2 * S5_HALF, S5_CHUNK * S5_GROUP), lambda s, i: (s, 0, 0)),
                  _const_spec((S5_CHUNK * S5_GROUP, S5_FOLD)),
                  pl.BlockSpec((1, LANES), lambda s, i: (0, s))],
        out_specs=xspec,
        scratch_shapes=[pltpu.VMEM((S5_FOLD, S5_FOLD), BF16), pltpu.VMEM((2 * S5_HALF, S5_FOLD), BF16)],
        compiler_params=_cparams("arbitrary", "arbitrary"),
        name="s5_out",
    )(h, hr.reshape(nrow, states), hi.reshape(nrow, states), bd_op, q_op, rep_q, d_skip.reshape(1, -1))
    row = lambda width: pl.BlockSpec((OUT_ROWS, width), lambda i: (i, 0))
    return pl.pallas_call(
        _glu_ln_kernel,
        out_shape=jax.ShapeDtypeStruct((n, D_MODEL), F32),
        grid=(n // OUT_ROWS,),
        in_specs=[row(D_MODEL), row(D_MODEL), _const_spec((D_MODEL, 2 * D_MODEL)),
                  _const_spec((1, D_MODEL)), _const_spec((1, D_MODEL))],
        out_specs=row(D_MODEL),
        compiler_params=_cparams("parallel"),
        name="s5_glu_ln",
    )(h, y, w_glu.astype(BF16), g.reshape(1, -1), b.reshape(1, -1))


def kernel(x, ffn1_w_in, ffn1_w_out, ffn2_w_in, ffn2_w_out, ln_g, ln_b, ev_w_in, ev_conv_w, ev_conv_b, ev_cln_g, ev_cln_b, ev_pe_k, ev_w1_k, ev_w2_k, ev_pe_v, ev_w1_v, ev_w2_v, ev_w_out, od_a_re, od_a_im, od_log_dt, od_b_re, od_b_im, od_c_re, od_c_im, od_d, od_w_glu):
    bsz, seq, dm = x.shape
    assert dm == D_MODEL and seq % PROJ_ROWS == 0 and seq >= 2 * WINDOW
    n = bsz * seq
    h = x.reshape(n, dm)
    for layer in range(DEPTH):
        i = layer // 2
        h = _ffn_ln(h, ffn1_w_in[layer], ffn1_w_out[layer], ln_g[layer, 0], ln_b[layer, 0])
        if layer % 2 == 0:
            a, kcr, vcr, ksel, kwin, qt, vt, gt = _even_proj(h.reshape(bsz, seq, dm), ev_w_in[i])
            a = _conv_module(a, ev_conv_w[i], ev_conv_b[i], ev_cln_g[i], ev_cln_b[i])
            kc, vct = _compress(kcr, vcr, ev_pe_k[i], ev_w1_k[i], ev_w2_k[i],
                                ev_pe_v[i], ev_w1_v[i], ev_w2_v[i])
            o = _nsa(qt, kc, vct, ksel, kwin, vt, gt)
            h = _outproj_ln(h, a.reshape(n, CONV_DIM), o.reshape(n, QCOLS), ev_w_out[i],
                            ln_g[layer, 1], ln_b[layer, 1])
        else:
            params = (od_a_re[i], od_a_im[i], od_log_dt[i], od_b_re[i], od_b_im[i],
                      od_c_re[i], od_c_im[i])
            h = _s5_mixer_ln(h, bsz, params, od_d[i], od_w_glu[i], ln_g[layer, 1], ln_b[layer, 1])
        h = _ffn_ln(h, ffn2_w_in[layer], ffn2_w_out[layer], ln_g[layer, 2], ln_b[layer, 2])
    return h.reshape(bsz, seq, dm)
```

```python
import jax
import jax.numpy as jnp
import numpy as np
from jax import lax
from jax.experimental import pallas as pl
from jax.experimental.pallas import tpu as pltpu

F32 = jnp.float32
BF16 = jnp.bfloat16

D_MODEL = 1024
DEPTH = 4
FFN_DIM = ((8 * D_MODEL // 3 + 127) // 128) * 128
DN_ALPHA = (2.0 * DEPTH) ** 0.25
LN_EPS = 1e-5
CONV_DIM = D_MODEL // 2
CONV_WIDTH = 31
HEAD_DIM = 64
N_HEADS = (D_MODEL // 2) // HEAD_DIM
KV_GROUPS = max(1, N_HEADS // 4)
HPG = N_HEADS // KV_GROUPS
CMP_LEN = 32
CMP_STRIDE = 16
CMP_HIDDEN = 128
SEL_BLOCK = 64
SEL_TOPN = 16
WINDOW = 512
Q_BLOCK = 128
FORCE_SCORE = 1e4
S5_GROUP = 16
S5_GROUPS = D_MODEL // S5_GROUP
S5_STATE = 64
S5_CHUNK = 16
S5_SUPER = 8
N_SUPER = S5_GROUPS // S5_SUPER

LANES = 128
SUBLANES = 8
VMEM_LIMIT_BYTES = 56 * 1024 * 1024

NEG_BIG = -3e38
MASK_NEG = -2.0 ** 100
FEAT_SPLIT = 3
FEAT_COLS = 16
NULL_COL = HEAD_DIM + 4 * FEAT_SPLIT
LOG2E = 1.4426950408889634
KEY_TILE = 128
QROWS = HPG * Q_BLOCK
WIN_TILES = WINDOW // KEY_TILE + 1
SEL_SHIFT = SEL_BLOCK.bit_length() - 1
TILE_BLOCKS = KEY_TILE // SEL_BLOCK
WORD_BLOCKS = 16
SEL_UNROLL = 12


def _cparams(*sem):
    return pltpu.CompilerParams(dimension_semantics=sem, vmem_limit_bytes=VMEM_LIMIT_BYTES)


def _const_spec(shape):
    nd = len(shape)
    return pl.BlockSpec(shape, lambda *_: (0,) * nd)


def _layer_norm(y, g, b):
    mu = jnp.mean(y, axis=-1, keepdims=True)
    yc = y - mu
    var = jnp.mean(yc * yc, axis=-1, keepdims=True)
    return yc * lax.rsqrt(var + LN_EPS) * g + b


FFN_ROWS = 512
FFN_CHUNK = 256


def _ffn_kernel(x_ref, win_ref, wout_ref, g_ref, b_ref, o_ref):
    x = x_ref[...]
    xb = x.astype(BF16)
    acc = None
    for c in range(FFN_DIM // FFN_CHUNK):
        lo = c * FFN_CHUNK
        gate = jnp.dot(xb, win_ref[:, lo:lo + FFN_CHUNK], preferred_element_type=F32)
        val = jnp.dot(xb, win_ref[:, FFN_DIM + lo:FFN_DIM + lo + FFN_CHUNK],
                      preferred_element_type=F32)
        act = (gate * jax.nn.sigmoid(gate) * val).astype(BF16)
        part = jnp.dot(act, wout_ref[lo:lo + FFN_CHUNK, :], preferred_element_type=F32)
        acc = part if acc is None else acc + part
    o_ref[...] = _layer_norm(DN_ALPHA * x + 0.5 * acc, g_ref[...], b_ref[...])


def _ffn_ln(h, w_in, w_out, g, b):
    n = h.shape[0]
    return pl.pallas_call(
        _ffn_kernel,
        out_shape=jax.ShapeDtypeStruct((n, D_MODEL), F32),
        grid=(n // FFN_ROWS,),
        in_specs=[
            pl.BlockSpec((FFN_ROWS, D_MODEL), lambda i: (i, 0)),
            _const_spec((D_MODEL, 2 * FFN_DIM)),
            _const_spec((FFN_DIM, D_MODEL)),
            _const_spec((1, D_MODEL)),
            _const_spec((1, D_MODEL)),
        ],
        out_specs=pl.BlockSpec((FFN_ROWS, D_MODEL), lambda i: (i, 0)),
        compiler_params=_cparams("parallel"),
        name="ffn_ln",
    )(h, w_in.astype(BF16), w_out.astype(BF16), g.reshape(1, -1), b.reshape(1, -1))


PROJ_ROWS = 512
GD = KV_GROUPS * HEAD_DIM
QCOLS = N_HEADS * HEAD_DIM
SEL_K = 2 * LANES
WIN_K = LANES
NAT_COLS = 2 * CONV_DIM + 2 * GD + KV_GROUPS * 2 * HEAD_DIM
GATE_ROWS = 16
TR_ROWS = QCOLS + KV_GROUPS * 2 * HEAD_DIM + KV_GROUPS * GATE_ROWS


def _evproj_kernel(x_ref, wn_ref, wt_ref, kcs_ref, kcw_ref,
                   a_ref, kcr_ref, vcr_ref, ksel_ref, kwin_ref, qt_ref, vt_ref, gt_ref):
    xb = x_ref[0].astype(BF16)
    r = jnp.dot(xb, wn_ref[...], preferred_element_type=F32)
    a_ref[0] = r[:, :CONV_DIM] * jax.nn.sigmoid(r[:, CONV_DIM:2 * CONV_DIM])
    off = 2 * CONV_DIM
    kcr_ref[0] = r[:, off:off + GD]
    vcr_ref[0] = r[:, off + GD:off + 2 * GD]
    off += 2 * GD
    lane = lax.broadcasted_iota(jnp.int32, (r.shape[0], LANES), 1)
    feat = kcw_ref[...]
    for g in range(KV_GROUPS):
        blk = r[:, off + g * LANES:off + (g + 1) * LANES]
        ksel = jnp.where(lane < HEAD_DIM, blk, feat)
        kwin = jnp.where(lane < HEAD_DIM, pltpu.roll(blk, LANES - HEAD_DIM, 1), feat)
        ksel_ref[0, g] = jnp.concatenate([ksel, kcs_ref[...]], axis=1).astype(BF16)
        kwin_ref[0, g] = kwin.astype(BF16)
    rt = lax.dot_general(wt_ref[...], xb, (((1,), (1,)), ((), ())), preferred_element_type=F32)
    hq = HPG * HEAD_DIM
    for g in range(KV_GROUPS):
        qt_ref[0, g] = (rt[g * hq:(g + 1) * hq] * (HEAD_DIM ** -0.5 * LOG2E)).astype(BF16)
        lo = QCOLS + g * 128
        vt_ref[0, g] = rt[lo:lo + 128].astype(BF16)
        lo = QCOLS + KV_GROUPS * 128 + g * GATE_ROWS
        gt_ref[0, g] = jax.nn.sigmoid(rt[lo:lo + GATE_ROWS])


def _even_proj_weights(w_in):
    gd = GD
    o = 2 * CONV_DIM
    a_in = w_in[:, :o]
    q = w_in[:, o:o + QCOLS]
    o += QCOLS
    kc, vc, ks, vs, kw, vw = [w_in[:, o + i * gd:o + (i + 1) * gd] for i in range(6)]
    gates = w_in[:, o + 6 * gd:]
    nat = [a_in, kc, vc]
    tr = [q]
    for g in range(KV_GROUPS):
        sl = slice(g * HEAD_DIM, (g + 1) * HEAD_DIM)
        nat += [ks[:, sl], kw[:, sl]]
    for g in range(KV_GROUPS):
        sl = slice(g * HEAD_DIM, (g + 1) * HEAD_DIM)
        tr += [vs[:, sl], vw[:, sl]]
    for g in range(KV_GROUPS):
        gg = gates[:, g * 3 * HPG:(g + 1) * 3 * HPG]
        tr.append(jnp.pad(gg, ((0, 0), (0, GATE_ROWS - 3 * HPG))))
    wn = jnp.concatenate(nat, axis=1).astype(BF16)
    wt = jnp.concatenate(tr, axis=1).T.astype(BF16)
    return wn, wt


def _position_features(pos):
    feat = np.zeros((pos.shape[0], WIN_K), np.float32)
    n = FEAT_SPLIT
    feat[:, HEAD_DIM:HEAD_DIM + n] = (pos >> SEL_SHIFT)[:, None]
    feat[:, HEAD_DIM + n:HEAD_DIM + 2 * n] = (pos & (SEL_BLOCK - 1))[:, None]
    feat[:, HEAD_DIM + 2 * n:HEAD_DIM + 4 * n] = 1.0
    return feat


def _key_constants(s):
    pos = np.arange(s)
    feat = _position_features(pos)
    onehot = (pos[:, None] >> SEL_SHIFT == np.arange(SEL_K - WIN_K)[None, :]).astype(np.float32)
    return jnp.asarray(onehot), jnp.asarray(feat)


def _even_proj(h3, w_in):
    assert 2 * HEAD_DIM == LANES and WIN_K == LANES
    b, s, _ = h3.shape
    wn, wt = _even_proj_weights(w_in)
    kcs, kcw = _key_constants(s)
    nt = s // PROJ_ROWS
    rows = PROJ_ROWS
    return pl.pallas_call(
        _evproj_kernel,
        out_shape=[
            jax.ShapeDtypeStruct((b, s, CONV_DIM), F32),
            jax.ShapeDtypeStruct((b, s, GD), F32),
            jax.ShapeDtypeStruct((b, s, GD), F32),
            jax.ShapeDtypeStruct((b, KV_GROUPS, s, SEL_K), BF16),
            jax.ShapeDtypeStruct((b, KV_GROUPS, s, WIN_K), BF16),
            jax.ShapeDtypeStruct((b, KV_GROUPS, HPG * HEAD_DIM, s), BF16),
            jax.ShapeDtypeStruct((b, KV_GROUPS, 128, s), BF16),
            jax.ShapeDtypeStruct((b, KV_GROUPS, GATE_ROWS, s), F32),
        ],
        grid=(b, nt),
        in_specs=[
            pl.BlockSpec((1, rows, D_MODEL), lambda i, j: (i, j, 0)),
            _const_spec((D_MODEL, NAT_COLS)),
            _const_spec((TR_ROWS, D_MODEL)),
            pl.BlockSpec((rows, SEL_K - WIN_K), lambda i, j: (j, 0)),
            pl.BlockSpec((rows, WIN_K), lambda i, j: (j, 0)),
        ],
        out_specs=[
            pl.BlockSpec((1, rows, CONV_DIM), lambda i, j: (i, j, 0)),
            pl.BlockSpec((1, rows, GD), lambda i, j: (i, j, 0)),
            pl.BlockSpec((1, rows, GD), lambda i, j: (i, j, 0)),
            pl.BlockSpec((1, KV_GROUPS, rows, SEL_K), lambda i, j: (i, 0, j, 0)),
            pl.BlockSpec((1, KV_GROUPS, rows, WIN_K), lambda i, j: (i, 0, j, 0)),
            pl.BlockSpec((1, KV_GROUPS, HPG * HEAD_DIM, rows), lambda i, j: (i, 0, 0, j)),
            pl.BlockSpec((1, KV_GROUPS, 128, rows), lambda i, j: (i, 0, 0, j)),
            pl.BlockSpec((1, KV_GROUPS, GATE_ROWS, rows), lambda i, j: (i, 0, 0, j)),
        ],
        compiler_params=_cparams("parallel", "parallel"),
        name="even_proj",
    )(h3, wn, wt, kcs, kcw)


CONV_ROWS = 256
CONV_HALO = 32
CONV_SUB = 128


def _conv_kernel(cur_ref, halo_ref, w_ref, cb_ref, g_ref, b_ref, o_ref, buf_ref, acc_ref, sh_ref):
    first = pl.program_id(1) == 0
    buf_ref[0:CONV_HALO, :] = jnp.where(first, 0.0, halo_ref[0])
    buf_ref[CONV_HALO:CONV_HALO + CONV_ROWS, :] = cur_ref[0]
    lead = CONV_HALO - (CONV_WIDTH - 1)
    for c in range(CONV_DIM // LANES):
        cols = slice(c * LANES, (c + 1) * LANES)
        for rc in range(CONV_ROWS // CONV_SUB):
            acc = jnp.zeros((CONV_SUB, LANES), F32) + cb_ref[:, cols]
            for r in range(SUBLANES):
                base = lead + r + rc * CONV_SUB
                taps = range(r, CONV_WIDTH, SUBLANES)
                span = CONV_SUB + (len(taps) - 1) * SUBLANES
                sh_ref[0:span, :] = buf_ref[base:base + span, cols]
                for k in taps:
                    acc = acc + w_ref[k][:, cols] * sh_ref[k - r:k - r + CONV_SUB, :]
            acc_ref[rc * CONV_SUB:(rc + 1) * CONV_SUB, cols] = acc
    y = _layer_norm(acc_ref[...], g_ref[...], b_ref[...])
    o_ref[0] = (y * jax.nn.sigmoid(y)).astype(BF16)


def _conv_module(a, conv_w, conv_b, cln_g, cln_b):
    b, s, _ = a.shape
    per = CONV_ROWS // CONV_HALO
    return pl.pallas_call(
        _conv_kernel,
        out_shape=jax.ShapeDtypeStruct((b, s, CONV_DIM), BF16),
        grid=(b, s // CONV_ROWS),
        in_specs=[
            pl.BlockSpec((1, CONV_ROWS, CONV_DIM), lambda i, j: (i, j, 0)),
            pl.BlockSpec((1, CONV_HALO, CONV_DIM), lambda i, j: (i, jnp.maximum(j * per - 1, 0), 0)),
            _const_spec((CONV_WIDTH, 1, CONV_DIM)),
            _const_spec((1, CONV_DIM)),
            _const_spec((1, CONV_DIM)),
            _const_spec((1, CONV_DIM)),
        ],
        out_specs=pl.BlockSpec((1, CONV_ROWS, CONV_DIM), lambda i, j: (i, j, 0)),
        scratch_shapes=[pltpu.VMEM((CONV_HALO + CONV_ROWS, CONV_DIM), F32),
                        pltpu.VMEM((CONV_ROWS, CONV_DIM), F32),
                        pltpu.VMEM((CONV_SUB + (-(-CONV_WIDTH // SUBLANES) - 1) * SUBLANES, LANES), F32)],
        compiler_params=_cparams("parallel", "arbitrary"),
        name="conv_module",
    )(a, a, conv_w.reshape(CONV_WIDTH, 1, CONV_DIM), conv_b.reshape(1, -1),
      cln_g.reshape(1, -1), cln_b.reshape(1, -1))


def _compress_one(x_ref, pe_ref, w1_ref, w2_ref):
    nch = x_ref.shape[1]
    first = None
    second = None
    for l in range(CMP_STRIDE):
        xl = x_ref[0, :, l, :]
        pa = jnp.dot((xl + pe_ref[l]).astype(BF16), w1_ref[l], preferred_element_type=F32)
        pb = jnp.dot((xl + pe_ref[CMP_STRIDE + l]).astype(BF16), w1_ref[CMP_STRIDE + l],
                     preferred_element_type=F32)
        first = pa if first is None else first + pa
        second = pb if second is None else second + pb
    u = first + pltpu.roll(second, nch - 1, 0)
    hid = (u * jax.nn.sigmoid(u)).astype(BF16)
    out = jnp.dot(hid, w2_ref[...], preferred_element_type=F32)
    row = lax.broadcasted_iota(jnp.int32, out.shape, 0)
    return jnp.where(row < nch - 1, out, 0.0)


def _compress_kernel(xk_ref, xv_ref, pek_ref, w1k_ref, w2k_ref, pev_ref, w1v_ref, w2v_ref, feat_ref,
                     kc_ref, vct_ref):
    kc = _compress_one(xk_ref, pek_ref, w1k_ref, w2k_ref)
    vct = _compress_one(xv_ref, pev_ref, w1v_ref, w2v_ref).T
    lane = lax.broadcasted_iota(jnp.int32, kc.shape, 1)
    for g in range(KV_GROUPS):
        keys = kc if g == 0 else pltpu.roll(kc, GD - g * HEAD_DIM, 1)
        kc_ref[0, g] = jnp.where(lane < HEAD_DIM, keys, feat_ref[...]).astype(BF16)
        vct_ref[0, g] = vct[g * HEAD_DIM:(g + 1) * HEAD_DIM, :].astype(BF16)


def _compress_weights(pe, w1, w2):
    eye = jnp.eye(KV_GROUPS, dtype=F32)
    pe2 = jnp.tile(pe, (1, KV_GROUPS)).reshape(CMP_LEN, 1, GD)
    w1r = w1.reshape(CMP_LEN, HEAD_DIM, CMP_HIDDEN)
    w1bd = jnp.einsum('ldh,gk->lgdkh', w1r, eye).reshape(CMP_LEN, GD, KV_GROUPS * CMP_HIDDEN)
    w2bd = jnp.einsum('hd,gk->ghkd', w2, eye).reshape(KV_GROUPS * CMP_HIDDEN, GD)
    return pe2, w1bd.astype(BF16), w2bd.astype(BF16)


def _compress(kcr, vcr, pe_k, w1_k, w2_k, pe_v, w1_v, w2_v):
    b, s, _ = kcr.shape
    nch = s // CMP_STRIDE
    xk = kcr.reshape(b, nch, CMP_STRIDE, GD)
    xv = vcr.reshape(b, nch, CMP_STRIDE, GD)
    wk = _compress_weights(pe_k, w1_k, w2_k)
    wv = _compress_weights(pe_v, w1_v, w2_v)
    xspec = pl.BlockSpec((1, nch, CMP_STRIDE, GD), lambda i: (i, 0, 0, 0))
    wspecs = [_const_spec((CMP_LEN, 1, GD)), _const_spec((CMP_LEN, GD, KV_GROUPS * CMP_HIDDEN)),
              _const_spec((KV_GROUPS * CMP_HIDDEN, GD))]
    assert GD == WIN_K
    feat = jnp.asarray(_position_features(np.arange(nch) * CMP_STRIDE + CMP_LEN - 1))
    return pl.pallas_call(
        _compress_kernel,
        out_shape=[jax.ShapeDtypeStruct((b, KV_GROUPS, nch, WIN_K), BF16),
                   jax.ShapeDtypeStruct((b, KV_GROUPS, HEAD_DIM, nch), BF16)],
        grid=(b,),
        in_specs=[xspec, xspec] + wspecs + wspecs + [_const_spec((nch, WIN_K))],
        out_specs=[pl.BlockSpec((1, KV_GROUPS, nch, WIN_K), lambda i: (i, 0, 0, 0)),
                   pl.BlockSpec((1, KV_GROUPS, HEAD_DIM, nch), lambda i: (i, 0, 0, 0))],
        compiler_params=_cparams("parallel"),
        name="compress",
    )(xk, xv, *wk, *wv, feat)


def _tile_scores(tiles, q_rhs):
    scores = []
    for k, _, mask in tiles:
        s = jnp.dot(k, q_rhs, preferred_element_type=F32)
        scores.append(s if mask is None else s + mask)
    return scores


def _softmax_update(carry, scores, tiles):
    m, l, acc = carry
    m_new = m
    for s in scores:
        m_new = jnp.maximum(m_new, jnp.max(s, axis=0, keepdims=True))
    probs = [jnp.exp2(s - m_new) for s in scores]
    alpha = jnp.exp2(m - m_new)
    l = alpha * l
    for p in probs:
        l = l + jnp.sum(p, axis=0, keepdims=True)
    p_all = jnp.concatenate([p.astype(BF16) for p in probs], axis=0)
    v_all = jnp.concatenate([v for _, v, _ in tiles], axis=1)
    acc = alpha * acc + jnp.dot(v_all, p_all, preferred_element_type=F32)
    return m_new, l, acc


def _attend_step(carry, tiles, q_rhs):
    return _softmax_update(carry, _tile_scores(tiles, q_rhs), tiles)


def _nsa_kernel(qt_ref, kc_ref, vct_ref, ksel_ref, kwin_ref, vt_ref, gt_ref, ovt_ref, tab_ref,
                knull_ref, slope_ref, o_ref, qrhs_ref, words_ref, list_ref):
    qb = pl.program_id(2)
    t0 = qb * Q_BLOCK
    nsel = ovt_ref.shape[0]
    qt = qt_ref[0, 0]
    qcat = jnp.concatenate([qt[j * HEAD_DIM:(j + 1) * HEAD_DIM, :] for j in range(HPG)], axis=1)
    slope = slope_ref[0]

    qpos_row = t0 + (lax.broadcasted_iota(jnp.int32, (1, QROWS), 1) & (Q_BLOCK - 1))
    q_hi = (qpos_row >> SEL_SHIFT).astype(F32)
    q_lo = (qpos_row & (SEL_BLOCK - 1)).astype(F32)
    coef = slope * LOG2E

    def split(x):
        rows = []
        for _ in range(FEAT_SPLIT):
            part = x.astype(BF16).astype(F32)
            rows.append(part)
            x = x - part
        return rows

    feat = jnp.concatenate(split(coef * SEL_BLOCK) + split(coef) + split(-(coef * SEL_BLOCK) * q_hi)
                           + split(-coef * q_lo) + [jnp.full((1, QROWS), MASK_NEG, F32)]
                           + [jnp.zeros((FEAT_COLS - 4 * FEAT_SPLIT - 1, QROWS), F32)], axis=0)
    q_win = jnp.concatenate([qcat.astype(F32), feat,
                             jnp.zeros((WIN_K - HEAD_DIM - FEAT_COLS, QROWS), F32)], axis=0).astype(BF16)
    qrhs_ref[0:WIN_K, :] = q_win

    init = (jnp.full((1, QROWS), NEG_BIG, F32), jnp.zeros((1, QROWS), F32),
            jnp.zeros((HEAD_DIM, QROWS), F32))

    def key_tile(k_ref, t_raw, v_lo, mask=None, maybe_null=True):
        k0 = pl.multiple_of(jnp.maximum(t_raw, 0) * KEY_TILE, KEY_TILE)
        keys = k_ref[0, 0, pl.ds(k0, KEY_TILE), :]
        if maybe_null:
            keys = jnp.where(t_raw < 0, knull_ref[:, 0:keys.shape[1]], keys)
        return keys, vt_ref[0, 0, v_lo:v_lo + HEAD_DIM, pl.ds(k0, KEY_TILE)], mask

    nc = kc_ref.shape[2]
    s = jnp.dot(kc_ref[0, 0], q_win, preferred_element_type=F32)
    win_tiles = []
    for w in range(WIN_TILES):
        edge = tab_ref[0] if w == 0 else (tab_ref[1] if w == WIN_TILES - 1 else None)
        win_tiles.append(key_tile(kwin_ref, qb - (WIN_TILES - 1) + w, HEAD_DIM, edge, w < WIN_TILES - 1))
    win_scores = _tile_scores(win_tiles, q_win)
    cend = lax.broadcasted_iota(jnp.int32, (nc, QROWS), 0) * CMP_STRIDE + (CMP_LEN - 1)
    qpos = t0 + (lax.broadcasted_iota(jnp.int32, (nc, QROWS), 1) & (Q_BLOCK - 1))
    s = jnp.where(qpos >= cend, s, -jnp.inf)
    m_c = jnp.maximum(jnp.max(s, axis=0, keepdims=True), NEG_BIG)
    p = jnp.exp2(s - m_c)
    l_c = jnp.sum(p, axis=0, keepdims=True)
    pb = p.astype(BF16)
    inv_c = jnp.where(l_c > 0, 1.0 / l_c, 0.0)
    o_cmp = jnp.dot(vct_ref[0, 0], pb, preferred_element_type=F32) * inv_c
    imp4 = jnp.dot(ovt_ref[...], pb, preferred_element_type=F32) * inv_c
    imp = imp4[:, 0:Q_BLOCK]
    for j in range(1, HPG):
        imp = imp + imp4[:, j * Q_BLOCK:(j + 1) * Q_BLOCK]

    _, l_w, acc_w = _softmax_update(init, win_scores, win_tiles)

    n_iota = lax.broadcasted_iota(jnp.int32, (nsel, Q_BLOCK), 0)
    cur = (t0 + lax.broadcasted_iota(jnp.int32, (nsel, Q_BLOCK), 1)) >> SEL_SHIFT
    valid = n_iota <= cur
    forced = (n_iota == 0) | (n_iota == cur) | (n_iota == cur - 1)
    score = jnp.where(valid, jnp.where(forced, FORCE_SCORE, imp), -1.0)
    selm = jnp.full((nsel, Q_BLOCK), MASK_NEG, F32)

    def pick(carry):
        score, selm = carry
        best = jnp.max(score, axis=0, keepdims=True)
        first = jnp.min(jnp.where(score == best, n_iota, nsel), axis=0, keepdims=True)
        hit = n_iota == first
        selm = jnp.where(hit & (best >= 0), 0.0, selm)
        return jnp.where(hit, -2.0, score), selm

    carry = (score, selm)
    for _ in range(min(SEL_TOPN, nsel)):
        carry = pick(carry)
    selm = carry[1]
    qrhs_ref[WIN_K:WIN_K + nsel, :] = jnp.concatenate([selm] * HPG, axis=1).astype(BF16)
    if nsel < SEL_K - WIN_K:
        qrhs_ref[WIN_K + nsel:SEL_K, :] = jnp.zeros((SEL_K - WIN_K - nsel, QROWS), BF16)

    active = jnp.max(jnp.where(selm == 0.0, 1.0, 0.0), axis=1, keepdims=True)
    weight = jnp.left_shift(1, lax.broadcasted_iota(jnp.int32, (nsel, 1), 0) & (WORD_BLOCKS - 1))
    packed = active * weight.astype(F32)
    for w in range(nsel // WORD_BLOCKS):
        words_ref[w] = jnp.sum(packed[w * WORD_BLOCKS:(w + 1) * WORD_BLOCKS, :]).astype(jnp.int32)

    tiles_per_word = WORD_BLOCKS // TILE_BLOCKS

    def compact(w, n):
        word = words_ref[w]
        for u in range(tiles_per_word):
            t = w * tiles_per_word + u
            hit = (((word >> (u * TILE_BLOCKS)) & ((1 << TILE_BLOCKS) - 1)) != 0) & (t < qb)
            list_ref[n] = t
            n = n + hit.astype(jnp.int32)
        return n

    n_act = lax.fori_loop(0, (qb >> (tiles_per_word.bit_length() - 1)) + 1, compact, 0)

    def sel_step(i, carry):
        tiles = [key_tile(ksel_ref, list_ref[i * SEL_UNROLL + u], 0, None, False) for u in range(SEL_UNROLL)]
        return _attend_step(carry, tiles, qrhs_ref[...])

    n_full = n_act // SEL_UNROLL
    carry = lax.fori_loop(0, n_full, sel_step, init)
    last = jnp.maximum(n_act - 1, 0)
    tiles = []
    for u in range(SEL_UNROLL - 1):
        idx = n_full * SEL_UNROLL + u
        tiles.append(key_tile(ksel_ref, jnp.where(idx < n_act, list_ref[jnp.minimum(idx, last)], -1), 0))
    tiles.append(key_tile(ksel_ref, qb, 0, tab_ref[1], False))
    _, l_s, acc_s = _attend_step(carry, tiles, qrhs_ref[...])

    gt = gt_ref[0, 0]

    def gate_row(branch):
        return jnp.concatenate([gt[j * 3 + branch:j * 3 + branch + 1, :] for j in range(HPG)], axis=1)

    o_t = gate_row(0) * o_cmp + gate_row(1) * (acc_s / l_s) + gate_row(2) * (acc_w / l_w)
    out = jnp.concatenate([o_t[:, j * Q_BLOCK:(j + 1) * Q_BLOCK].T for j in range(HPG)], axis=1)
    o_ref[0] = out.astype(BF16)


def _nsa_tables(s):
    heads = np.arange(1, N_HEADS + 1, dtype=np.float32)
    slopes = (2.0 ** (-8.0 * heads / N_HEADS)).astype(np.float32).reshape(KV_GROUPS, HPG)
    slope_rows = np.repeat(slopes, Q_BLOCK, axis=1).reshape(KV_GROUPS, 1, QROWS)
    i = np.arange(KEY_TILE)[:, None]
    ql = np.tile(np.arange(Q_BLOCK), HPG)[None, :]
    neg = np.float32(-np.inf)
    tabs = np.stack([np.where(i > ql, np.float32(0), neg), np.where(i <= ql, np.float32(0), neg)], axis=0)
    knull = np.zeros((KEY_TILE, SEL_K), np.float32)
    knull[:, NULL_COL] = 1.0
    nch = s // CMP_STRIDE
    nsel = s // SEL_BLOCK
    c = np.arange(nch)
    n = np.arange(nsel)
    ov = ((c[None, :] * CMP_STRIDE + CMP_LEN - 1 >= n[:, None] * SEL_BLOCK)
          & (c[None, :] * CMP_STRIDE < (n[:, None] + 1) * SEL_BLOCK) & (c[None, :] < nch - 1))
    return jnp.asarray(slope_rows), jnp.asarray(tabs), jnp.asarray(knull, BF16), jnp.asarray(ov, BF16)


def _nsa(qt, kc, vct, ksel, kwin, vt, gt):
    b, _, _, s = qt.shape
    nch = s // CMP_STRIDE
    nsel = s // SEL_BLOCK
    assert nsel <= SEL_K - WIN_K and nsel % WORD_BLOCKS == 0
    slope_rows, tabs, knull, ovt = _nsa_tables(s)
    hq = HPG * HEAD_DIM
    per_group = lambda *shape: pl.BlockSpec((1, 1) + shape, lambda i, g, q: (i, g, 0, 0))
    return pl.pallas_call(
        _nsa_kernel,
        out_shape=jax.ShapeDtypeStruct((b, s, N_HEADS * HEAD_DIM), BF16),
        grid=(b, KV_GROUPS, s // Q_BLOCK),
        in_specs=[
            pl.BlockSpec((1, 1, hq, Q_BLOCK), lambda i, g, q: (i, g, 0, q)),
            per_group(nch, WIN_K),
            per_group(HEAD_DIM, nch),
            per_group(s, SEL_K),
            per_group(s, WIN_K),
            per_group(2 * HEAD_DIM, s),
            pl.BlockSpec((1, 1, GATE_ROWS, Q_BLOCK), lambda i, g, q: (i, g, 0, q)),
            _const_spec((nsel, nch)),
            _const_spec((2, KEY_TILE, QROWS)),
            _const_spec((KEY_TILE, SEL_K)),
            pl.BlockSpec((1, 1, QROWS), lambda i, g, q: (g, 0, 0)),
        ],
        out_specs=pl.BlockSpec((1, Q_BLOCK, hq), lambda i, g, q: (i, q, g)),
        scratch_shapes=[pltpu.VMEM((SEL_K, QROWS), BF16),
                        pltpu.SMEM((nsel // WORD_BLOCKS,), jnp.int32),
                        pltpu.SMEM((s // KEY_TILE + WORD_BLOCKS // TILE_BLOCKS,), jnp.int32)],
        compiler_params=_cparams("parallel", "parallel", "arbitrary"),
        name="nsa_attention",
    )(qt, kc, vct, ksel, kwin, vt, gt, ovt, tabs, knull, slope_rows)


OUT_ROWS = 512


def _outproj_kernel(h_ref, a_ref, o_ref, wa_ref, wo_ref, g_ref, b_ref, out_ref):
    m = (jnp.dot(a_ref[...], wa_ref[...], preferred_element_type=F32)
         + jnp.dot(o_ref[...], wo_ref[...], preferred_element_type=F32))
    out_ref[...] = _layer_norm(DN_ALPHA * h_ref[...] + m, g_ref[...], b_ref[...])


def _outproj_ln(h, a, o, w_out, g, b):
    n = h.shape[0]
    wa = w_out[:CONV_DIM].astype(BF16)
    wo = w_out[CONV_DIM:].astype(BF16)
    row = lambda width: pl.BlockSpec((OUT_ROWS, width), lambda i: (i, 0))
    return pl.pallas_call(
        _outproj_kernel,
        out_shape=jax.ShapeDtypeStruct((n, D_MODEL), F32),
        grid=(n // OUT_ROWS,),
        in_specs=[row(D_MODEL), row(CONV_DIM), row(QCOLS),
                  _const_spec((CONV_DIM, D_MODEL)), _const_spec((QCOLS, D_MODEL)),
                  _const_spec((1, D_MODEL)), _const_spec((1, D_MODEL))],
        out_specs=row(D_MODEL),
        compiler_params=_cparams("parallel"),
        name="mixer_out_ln",
    )(h, a, o, wa, wo, g.reshape(1, -1), b.reshape(1, -1))


S5_ROWS = 256
S5_FOLD = S5_CHUNK * LANES
S5_HALF = S5_SUPER * S5_STATE
SCAN_COLS = 512
S5_OUT_COLS = 256


def _s5_operators(a_re, a_im, log_dt, b_re, b_im, c_re, c_im):
    hi = lax.Precision.HIGHEST
    dt = jnp.exp(log_dt)[:, None]
    ar, ai = a_re, a_im
    mag = jnp.exp(ar * dt)
    lr, li = mag * jnp.cos(ai * dt), mag * jnp.sin(ai * dt)
    den = ar * ar + ai * ai
    zr = ((lr - 1.0) * ar + li * ai) / den
    zi = (li * ar - (lr - 1.0) * ai) / den
    bbr = zr[..., None] * b_re - zi[..., None] * b_im
    bbi = zr[..., None] * b_im + zi[..., None] * b_re
    tau = jnp.arange(S5_CHUNK + 1, dtype=F32)[:, None, None]
    pmag = jnp.exp(ar * dt * tau)
    pr, pi = pmag * jnp.cos(ai * dt * tau), pmag * jnp.sin(ai * dt * tau)
    clr = c_re[None] * pr[:, :, None, :] - c_im[None] * pi[:, :, None, :]
    cli = c_re[None] * pi[:, :, None, :] + c_im[None] * pr[:, :, None, :]
    kern = (jnp.einsum('tgxp,gpc->gtxc', clr[:S5_CHUNK], bbr, precision=hi)
            - jnp.einsum('tgxp,gpc->gtxc', cli[:S5_CHUNK], bbi, precision=hi))
    eye = jnp.eye(S5_SUPER, dtype=F32)
    k8 = kern.reshape(N_SUPER, S5_SUPER, S5_CHUNK, S5_GROUP, S5_GROUP)
    bd = (k8.transpose(0, 2, 1, 4, 3)[:, :, :, :, None, :] * eye[None, None, :, None, :, None])
    bd = bd.reshape(N_SUPER, S5_CHUNK, LANES, LANES)
    rev = S5_CHUNK - 1 - jnp.arange(S5_CHUNK)
    ppr = pr[rev][:, :, :, None] * bbr[None] - pi[rev][:, :, :, None] * bbi[None]
    ppi = pr[rev][:, :, :, None] * bbi[None] + pi[rev][:, :, :, None] * bbr[None]
    p8 = jnp.stack([ppr, ppi], axis=0).reshape(2, S5_CHUNK, N_SUPER, S5_SUPER, S5_STATE, S5_GROUP)
    p_op = p8.transpose(2, 1, 3, 5, 0, 4).reshape(N_SUPER, S5_FOLD, 2 * S5_STATE)
    q8 = jnp.stack([clr[1:], -cli[1:]], axis=0).reshape(2, S5_CHUNK, N_SUPER, S5_SUPER, S5_GROUP, S5_STATE)
    q_op = q8.transpose(2, 0, 3, 5, 1, 4).reshape(N_SUPER, 2 * S5_HALF, S5_CHUNK * S5_GROUP)
    dr = pr[S5_CHUNK].reshape(1, -1)
    di = pi[S5_CHUNK].reshape(1, -1)
    return bd.astype(BF16), p_op.astype(BF16), q_op.astype(BF16), dr, di


def _fold_rows(x_ref):
    chunks = x_ref.shape[0] // S5_CHUNK
    return [x_ref[pl.ds(l, chunks, stride=S5_CHUNK), :] for l in range(S5_CHUNK)]


EXPAND_ROWS = 256


def _expand_block_diag(dst_ref, src_ref, rep_ref, row_shift, col_shift):
    n_rows, n_cols = dst_ref.shape
    for r0 in range(0, n_rows, EXPAND_ROWS):
        full = jnp.dot(src_ref[0, r0:r0 + EXPAND_ROWS, :], rep_ref[...], preferred_element_type=F32)
        row_g = ((lax.broadcasted_iota(jnp.int32, full.shape, 0) + r0) >> row_shift) & (S5_SUPER - 1)
        col_g = (lax.broadcasted_iota(jnp.int32, full.shape, 1) >> col_shift) & (S5_SUPER - 1)
        dst_ref[r0:r0 + EXPAND_ROWS, :] = jnp.where(row_g == col_g, full, 0.0).astype(dst_ref.dtype)


def _s5_state_kernel(x_ref, p_ref, rep_ref, vr_ref, vi_ref, pfull_ref):
    @pl.when(pl.program_id(1) == 0)
    def _():
        _expand_block_diag(pfull_ref, p_ref, rep_ref, S5_GROUP.bit_length() - 1, S5_STATE.bit_length() - 1)

    x2 = jnp.concatenate(_fold_rows(x_ref), axis=1).astype(BF16)
    v = jnp.dot(x2, pfull_ref[...], preferred_element_type=F32)
    vr_ref[...] = v[:, :S5_HALF]
    vi_ref[...] = v[:, S5_HALF:]


def _s5_scan_kernel(vr_ref, vi_ref, dr_ref, di_ref, hr_ref, hi_ref):
    dr = dr_ref[...]
    di = di_ref[...]

    def body(k, carry):
        hr, hi = carry
        hr_ref[0, pl.ds(k, 1), :] = hr
        hi_ref[0, pl.ds(k, 1), :] = hi
        vr = vr_ref[0, pl.ds(k, 1), :]
        vi = vi_ref[0, pl.ds(k, 1), :]
        return dr * hr - di * hi + vr, dr * hi + di * hr + vi

    zero = jnp.zeros((1, SCAN_COLS), F32)
    lax.fori_loop(0, vr_ref.shape[1], body, (zero, zero))


def _s5_out_kernel(x_ref, hr_ref, hi_ref, bd_ref, q_ref, rep_ref, d_ref, y_ref, m_ref, qfull_ref):
    @pl.when(pl.program_id(1) == 0)
    def _():
        _expand_block_diag(qfull_ref, q_ref, rep_ref, S5_STATE.bit_length() - 1, S5_GROUP.bit_length() - 1)
        zero = jnp.zeros((LANES, LANES), m_ref.dtype)
        for l_in in range(S5_CHUNK):
            for l_out in range(S5_CHUNK):
                blk = bd_ref[0, l_out - l_in] if l_out >= l_in else zero
                m_ref[l_in * LANES:(l_in + 1) * LANES, l_out * LANES:(l_out + 1) * LANES] = blk

    xs = _fold_rows(x_ref)
    x2 = jnp.concatenate(xs, axis=1).astype(BF16)
    hcat = jnp.concatenate([hr_ref[...], hi_ref[...]], axis=1).astype(BF16)
    d = d_ref[...]
    chunks = y_ref.shape[0] // S5_CHUNK
    per = S5_OUT_COLS // LANES
    for j in range(S5_FOLD // S5_OUT_COLS):
        cols = slice(j * S5_OUT_COLS, (j + 1) * S5_OUT_COLS)
        kdim = (j + 1) * S5_OUT_COLS
        y = (jnp.dot(x2[:, :kdim], m_ref[:kdim, cols], preferred_element_type=F32)
             + jnp.dot(hcat, qfull_ref[:, cols], preferred_element_type=F32))
        for i in range(per):
            l = j * per + i
            y_ref[pl.ds(l, chunks, stride=S5_CHUNK), :] = jax.nn.gelu(y[:, i * LANES:(i + 1) * LANES] + d * xs[l])


def _glu_ln_kernel(h_ref, y_ref, w_ref, g_ref, b_ref, o_ref):
    r = jnp.dot(y_ref[...].astype(BF16), w_ref[...], preferred_element_type=F32)
    m = r[:, :D_MODEL] * jax.nn.sigmoid(r[:, D_MODEL:])
    o_ref[...] = _layer_norm(DN_ALPHA * h_ref[...] + m, g_ref[...], b_ref[...])


def _s5_mixer_ln(h, nbatch, params, d_skip, w_glu, g, b):
    n = h.shape[0]
    nrow = n // S5_CHUNK
    rows = min(S5_ROWS, nrow)
    bd_op, p_op, q_op, dr, di = _s5_operators(*params)
    xspec = pl.BlockSpec((rows * S5_CHUNK, LANES), lambda s, i: (i, s))
    half = pl.BlockSpec((rows, S5_HALF), lambda s, i: (i, s))
    states = N_SUPER * S5_HALF
    rep_p = np.zeros((2, S5_STATE, 2, S5_SUPER, S5_STATE), np.float32)
    rep_p[np.arange(2)[:, None], np.arange(S5_STATE)[None, :], np.arange(2)[:, None], :, np.arange(S5_STATE)[None, :]] = 1.0
    rep_p = jnp.asarray(rep_p.reshape(2 * S5_STATE, 2 * S5_HALF), BF16)
    rep_q = np.zeros((S5_CHUNK, S5_GROUP, S5_CHUNK, S5_SUPER, S5_GROUP), np.float32)
    rep_q[np.arange(S5_CHUNK)[:, None], np.arange(S5_GROUP)[None, :], np.arange(S5_CHUNK)[:, None], :, np.arange(S5_GROUP)[None, :]] = 1.0
    rep_q = jnp.asarray(rep_q.reshape(S5_CHUNK * S5_GROUP, S5_FOLD), BF16)
    vr, vi = pl.pallas_call(
        _s5_state_kernel,
        out_shape=[jax.ShapeDtypeStruct((nrow, states), F32)] * 2,
        grid=(N_SUPER, nrow // rows),
        in_specs=[xspec, pl.BlockSpec((1, S5_FOLD, 2 * S5_STATE), lambda s, i: (s, 0, 0)),
                  _const_spec((2 * S5_STATE, 2 * S5_HALF))],
        out_specs=[half, half],
        scratch_shapes=[pltpu.VMEM((S5_FOLD, 2 * S5_HALF), BF16)],
        compiler_params=_cparams("arbitrary", "arbitrary"),
        name="s5_state_in",
    )(h, p_op, rep_p)
    nch = nrow // nbatch
    seq = pl.BlockSpec((1, nch, SCAN_COLS), lambda i, c: (i, 0, c))
    dspec = pl.BlockSpec((1, SCAN_COLS), lambda i, c: (0, c))
    hr, hi = pl.pallas_call(
        _s5_scan_kernel,
        out_shape=[jax.ShapeDtypeStruct((nbatch, nch, states), F32)] * 2,
        grid=(nbatch, states // SCAN_COLS),
        in_specs=[seq, seq, dspec, dspec],
        out_specs=[seq, seq],
        compiler_params=_cparams("parallel", "parallel"),
        name="s5_scan",
    )(vr.reshape(nbatch, nch, states), vi.reshape(nbatch, nch, states), dr, di)
    y = pl.pallas_call(
        _s5_out_kernel,
        out_shape=jax.ShapeDtypeStruct((n, D_MODEL), F32),
        grid=(N_SUPER, nrow // rows),
        in_specs=[xspec, half, half,
                  pl.BlockSpec((1, S5_CHUNK, LANES, LANES), lambda s, i: (s, 0, 0, 0)),
                  pl.BlockSpec((1, 2 * S5_HALF, S5_CHUNK * S5_GROUP), lambda s, i: (s, 0, 0)),
                  _const_spec((S5_CHUNK * S5_GROUP, S5_FOLD)),
                  pl.BlockSpec((1, LANES), lambda s, i: (0, s))],
        out_specs=xspec,
        scratch_shapes=[pltpu.VMEM((S5_FOLD, S5_FOLD), BF16), pltpu.VMEM((2 * S5_HALF, S5_FOLD), BF16)],
        compiler_params=_cparams("arbitrary", "arbitrary"),
        name="s5_out",
    )(h, hr.reshape(nrow, states), hi.reshape(nrow, states), bd_op, q_op, rep_q, d_skip.reshape(1, -1))
    row = lambda width: pl.BlockSpec((OUT_ROWS, width), lambda i: (i, 0))
    return pl.pallas_call(
        _glu_ln_kernel,
        out_shape=jax.ShapeDtypeStruct((n, D_MODEL), F32),
        grid=(n // OUT_ROWS,),
        in_specs=[row(D_MODEL), row(D_MODEL), _const_spec((D_MODEL, 2 * D_MODEL)),
                  _const_spec((1, D_MODEL)), _const_spec((1, D_MODEL))],
        out_specs=row(D_MODEL),
        compiler_params=_cparams("parallel"),
        name="s5_glu_ln",
    )(h, y, w_glu.astype(BF16), g.reshape(1, -1), b.reshape(1, -1))


def kernel(x, ffn1_w_in, ffn1_w_out, ffn2_w_in, ffn2_w_out, ln_g, ln_b, ev_w_in, ev_conv_w, ev_conv_b, ev_cln_g, ev_cln_b, ev_pe_k, ev_w1_k, ev_w2_k, ev_pe_v, ev_w1_v, ev_w2_v, ev_w_out, od_a_re, od_a_im, od_log_dt, od_b_re, od_b_im, od_c_re, od_c_im, od_d, od_w_glu):
    bsz, seq, dm = x.shape
    assert dm == D_MODEL and seq % PROJ_ROWS == 0 and seq >= 2 * WINDOW
    n = bsz * seq
    h = x.reshape(n, dm)
    for layer in range(DEPTH):
        i = layer // 2
        h = _ffn_ln(h, ffn1_w_in[layer], ffn1_w_out[layer], ln_g[layer, 0], ln_b[layer, 0])
        if layer % 2 == 0:
            a, kcr, vcr, ksel, kwin, qt, vt, gt = _even_proj(h.reshape(bsz, seq, dm), ev_w_in[i])
            a = _conv_module(a, ev_conv_w[i], ev_conv_b[i], ev_cln_g[i], ev_cln_b[i])
            kc, vct = _compress(kcr, vcr, ev_pe_k[i], ev_w1_k[i], ev_w2_k[i],
                                ev_pe_v[i], ev_w1_v[i], ev_w2_v[i])
            o = _nsa(qt, kc, vct, ksel, kwin, vt, gt)
            h = _outproj_ln(h, a.reshape(n, CONV_DIM), o.reshape(n, QCOLS), ev_w_out[i],
                            ln_g[layer, 1], ln_b[layer, 1])
        else:
            params = (od_a_re[i], od_a_im[i], od_log_dt[i], od_b_re[i], od_b_im[i],
                      od_c_re[i], od_c_im[i])
            h = _s5_mixer_ln(h, bsz, params, od_d[i], od_w_glu[i], ln_g[layer, 1], ln_b[layer, 1])
        h = _ffn_ln(h, ffn2_w_in[layer], ffn2_w_out[layer], ln_g[layer, 2], ln_b[layer, 2])
    return h.reshape(bsz, seq, dm)
```

```python
import jax
import jax.numpy as jnp
import numpy as np
from jax import lax
from jax.experimental import pallas as pl
from jax.experimental.pallas import tpu as pltpu

F32 = jnp.float32
BF16 = jnp.bfloat16

D_MODEL = 1024
DEPTH = 4
FFN_DIM = ((8 * D_MODEL // 3 + 127) // 128) * 128
DN_ALPHA = (2.0 * DEPTH) ** 0.25
LN_EPS = 1e-5
CONV_DIM = D_MODEL // 2
CONV_WIDTH = 31
HEAD_DIM = 64
N_HEADS = (D_MODEL // 2) // HEAD_DIM
KV_GROUPS = max(1, N_HEADS // 4)
HPG = N_HEADS // KV_GROUPS
CMP_LEN = 32
CMP_STRIDE = 16
CMP_HIDDEN = 128
SEL_BLOCK = 64
SEL_TOPN = 16
WINDOW = 512
Q_BLOCK = 128
FORCE_SCORE = 1e4
S5_GROUP = 16
S5_GROUPS = D_MODEL // S5_GROUP
S5_STATE = 64
S5_CHUNK = 16
S5_SUPER = 8
N_SUPER = S5_GROUPS // S5_SUPER

LANES = 128
SUBLANES = 8
VMEM_LIMIT_BYTES = 56 * 1024 * 1024

NEG_BIG = -3e38
MASK_NEG = -2.0 ** 100
FEAT_SPLIT = 3
FEAT_COLS = 16
NULL_COL = HEAD_DIM + 4 * FEAT_SPLIT
LOG2E = 1.4426950408889634
KEY_TILE = 128
QROWS = HPG * Q_BLOCK
WIN_TILES = WINDOW // KEY_TILE + 1
SEL_SHIFT = SEL_BLOCK.bit_length() - 1
TILE_BLOCKS = KEY_TILE // SEL_BLOCK
WORD_BLOCKS = 16
SEL_UNROLL = 12


def _cparams(*sem):
    return pltpu.CompilerParams(dimension_semantics=sem, vmem_limit_bytes=VMEM_LIMIT_BYTES)


def _const_spec(shape):
    nd = len(shape)
    return pl.BlockSpec(shape, lambda *_: (0,) * nd)


def _layer_norm(y, g, b):
    mu = jnp.mean(y, axis=-1, keepdims=True)
    yc = y - mu
    var = jnp.mean(yc * yc, axis=-1, keepdims=True)
    return yc * lax.rsqrt(var + LN_EPS) * g + b


FFN_ROWS = 512
FFN_CHUNK = 256


def _ffn_kernel(x_ref, win_ref, wout_ref, g_ref, b_ref, o_ref):
    x = x_ref[...]
    xb = x.astype(BF16)
    acc = None
    for c in range(FFN_DIM // FFN_CHUNK):
        lo = c * FFN_CHUNK
        gate = jnp.dot(xb, win_ref[:, lo:lo + FFN_CHUNK], preferred_element_type=F32)
        val = jnp.dot(xb, win_ref[:, FFN_DIM + lo:FFN_DIM + lo + FFN_CHUNK],
                      preferred_element_type=F32)
        act = (gate * jax.nn.sigmoid(gate) * val).astype(BF16)
        part = jnp.dot(act, wout_ref[lo:lo + FFN_CHUNK, :], preferred_element_type=F32)
        acc = part if acc is None else acc + part
    o_ref[...] = _layer_norm(DN_ALPHA * x + 0.5 * acc, g_ref[...], b_ref[...])


def _ffn_ln(h, w_in, w_out, g, b):
    n = h.shape[0]
    return pl.pallas_call(
        _ffn_kernel,
        out_shape=jax.ShapeDtypeStruct((n, D_MODEL), F32),
        grid=(n // FFN_ROWS,),
        in_specs=[
            pl.BlockSpec((FFN_ROWS, D_MODEL), lambda i: (i, 0)),
            _const_spec((D_MODEL, 2 * FFN_DIM)),
            _const_spec((FFN_DIM, D_MODEL)),
            _const_spec((1, D_MODEL)),
            _const_spec((1, D_MODEL)),
        ],
        out_specs=pl.BlockSpec((FFN_ROWS, D_MODEL), lambda i: (i, 0)),
        compiler_params=_cparams("parallel"),
        name="ffn_ln",
    )(h, w_in.astype(BF16), w_out.astype(BF16), g.reshape(1, -1), b.reshape(1, -1))


PROJ_ROWS = 512
GD = KV_GROUPS * HEAD_DIM
QCOLS = N_HEADS * HEAD_DIM
SEL_K = 2 * LANES
WIN_K = LANES
NAT_COLS = 2 * CONV_DIM + 2 * GD + KV_GROUPS * 2 * HEAD_DIM
GATE_ROWS = 16
TR_ROWS = QCOLS + KV_GROUPS * 2 * HEAD_DIM + KV_GROUPS * GATE_ROWS


def _evproj_kernel(x_ref, wn_ref, wt_ref, kcs_ref, kcw_ref,
                   a_ref, kcr_ref, vcr_ref, ksel_ref, kwin_ref, qt_ref, vt_ref, gt_ref):
    xb = x_ref[0].astype(BF16)
    r = jnp.dot(xb, wn_ref[...], preferred_element_type=F32)
    a_ref[0] = r[:, :CONV_DIM] * jax.nn.sigmoid(r[:, CONV_DIM:2 * CONV_DIM])
    off = 2 * CONV_DIM
    kcr_ref[0] = r[:, off:off + GD]
    vcr_ref[0] = r[:, off + GD:off + 2 * GD]
    off += 2 * GD
    lane = lax.broadcasted_iota(jnp.int32, (r.shape[0], LANES), 1)
    feat = kcw_ref[...]
    for g in range(KV_GROUPS):
        blk = r[:, off + g * LANES:off + (g + 1) * LANES]
        ksel = jnp.where(lane < HEAD_DIM, blk, feat)
        kwin = jnp.where(lane < HEAD_DIM, pltpu.roll(blk, LANES - HEAD_DIM, 1), feat)
        ksel_ref[0, g] = jnp.concatenate([ksel, kcs_ref[...]], axis=1).astype(BF16)
        kwin_ref[0, g] = kwin.astype(BF16)
    rt = lax.dot_general(wt_ref[...], xb, (((1,), (1,)), ((), ())), preferred_element_type=F32)
    hq = HPG * HEAD_DIM
    for g in range(KV_GROUPS):
        qt_ref[0, g] = (rt[g * hq:(g + 1) * hq] * (HEAD_DIM ** -0.5 * LOG2E)).astype(BF16)
        lo = QCOLS + g * 128
        vt_ref[0, g] = rt[lo:lo + 128].astype(BF16)
        lo = QCOLS + KV_GROUPS * 128 + g * GATE_ROWS
        gt_ref[0, g] = jax.nn.sigmoid(rt[lo:lo + GATE_ROWS])


def _even_proj_weights(w_in):
    gd = GD
    o = 2 * CONV_DIM
    a_in = w_in[:, :o]
    q = w_in[:, o:o + QCOLS]
    o += QCOLS
    kc, vc, ks, vs, kw, vw = [w_in[:, o + i * gd:o + (i + 1) * gd] for i in range(6)]
    gates = w_in[:, o + 6 * gd:]
    nat = [a_in, kc, vc]
    tr = [q]
    for g in range(KV_GROUPS):
        sl = slice(g * HEAD_DIM, (g + 1) * HEAD_DIM)
        nat += [ks[:, sl], kw[:, sl]]
    for g in range(KV_GROUPS):
        sl = slice(g * HEAD_DIM, (g + 1) * HEAD_DIM)
        tr += [vs[:, sl], vw[:, sl]]
    for g in range(KV_GROUPS):
        gg = gates[:, g * 3 * HPG:(g + 1) * 3 * HPG]
        tr.append(jnp.pad(gg, ((0, 0), (0, GATE_ROWS - 3 * HPG))))
    wn = jnp.concatenate(nat, axis=1).astype(BF16)
    wt = jnp.concatenate(tr, axis=1).T.astype(BF16)
    return wn, wt


def _position_features(pos):
    feat = np.zeros((pos.shape[0], WIN_K), np.float32)
    n = FEAT_SPLIT
    feat[:, HEAD_DIM:HEAD_DIM + n] = (pos >> SEL_SHIFT)[:, None]
    feat[:, HEAD_DIM + n:HEAD_DIM + 2 * n] = (pos & (SEL_BLOCK - 1))[:, None]
    feat[:, HEAD_DIM + 2 * n:HEAD_DIM + 4 * n] = 1.0
    return feat


def _key_constants(s):
    pos = np.arange(s)
    feat = _position_features(pos)
    onehot = (pos[:, None] >> SEL_SHIFT == np.arange(SEL_K - WIN_K)[None, :]).astype(np.float32)
    return jnp.asarray(onehot), jnp.asarray(feat)


def _even_proj(h3, w_in):
    assert 2 * HEAD_DIM == LANES and WIN_K == LANES
    b, s, _ = h3.shape
    wn, wt = _even_proj_weights(w_in)
    kcs, kcw = _key_constants(s)
    nt = s // PROJ_ROWS
    rows = PROJ_ROWS
    return pl.pallas_call(
        _evproj_kernel,
        out_shape=[
            jax.ShapeDtypeStruct((b, s, CONV_DIM), F32),
            jax.ShapeDtypeStruct((b, s, GD), F32),
            jax.ShapeDtypeStruct((b, s, GD), F32),
            jax.ShapeDtypeStruct((b, KV_GROUPS, s, SEL_K), BF16),
            jax.ShapeDtypeStruct((b, KV_GROUPS, s, WIN_K), BF16),
            jax.ShapeDtypeStruct((b, KV_GROUPS, HPG * HEAD_DIM, s), BF16),
            jax.ShapeDtypeStruct((b, KV_GROUPS, 128, s), BF16),
            jax.ShapeDtypeStruct((b, KV_GROUPS, GATE_ROWS, s), F32),
        ],
        grid=(b, nt),
        in_specs=[
            pl.BlockSpec((1, rows, D_MODEL), lambda i, j: (i, j, 0)),
            _const_spec((D_MODEL, NAT_COLS)),
            _const_spec((TR_ROWS, D_MODEL)),
            pl.BlockSpec((rows, SEL_K - WIN_K), lambda i, j: (j, 0)),
            pl.BlockSpec((rows, WIN_K), lambda i, j: (j, 0)),
        ],
        out_specs=[
            pl.BlockSpec((1, rows, CONV_DIM), lambda i, j: (i, j, 0)),
            pl.BlockSpec((1, rows, GD), lambda i, j: (i, j, 0)),
            pl.BlockSpec((1, rows, GD), lambda i, j: (i, j, 0)),
            pl.BlockSpec((1, KV_GROUPS, rows, SEL_K), lambda i, j: (i, 0, j, 0)),
            pl.BlockSpec((1, KV_GROUPS, rows, WIN_K), lambda i, j: (i, 0, j, 0)),
            pl.BlockSpec((1, KV_GROUPS, HPG * HEAD_DIM, rows), lambda i, j: (i, 0, 0, j)),
            pl.BlockSpec((1, KV_GROUPS, 128, rows), lambda i, j: (i, 0, 0, j)),
            pl.BlockSpec((1, KV_GROUPS, GATE_ROWS, rows), lambda i, j: (i, 0, 0, j)),
        ],
        compiler_params=_cparams("parallel", "parallel"),
        name="even_proj",
    )(h3, wn, wt, kcs, kcw)


CONV_ROWS = 256
CONV_HALO = 32
CONV_SUB = 128


def _conv_kernel(cur_ref, halo_ref, w_ref, cb_ref, g_ref, b_ref, o_ref, buf_ref, acc_ref, sh_ref):
    first = pl.program_id(1) == 0
    buf_ref[0:CONV_HALO, :] = jnp.where(first, 0.0, halo_ref[0])
    buf_ref[CONV_HALO:CONV_HALO + CONV_ROWS, :] = cur_ref[0]
    lead = CONV_HALO - (CONV_WIDTH - 1)
    for c in range(CONV_DIM // LANES):
        cols = slice(c * LANES, (c + 1) * LANES)
        for rc in range(CONV_ROWS // CONV_SUB):
            acc = jnp.zeros((CONV_SUB, LANES), F32) + cb_ref[:, cols]
            for r in range(SUBLANES):
                base = lead + r + rc * CONV_SUB
                taps = range(r, CONV_WIDTH, SUBLANES)
                span = CONV_SUB + (len(taps) - 1) * SUBLANES
                sh_ref[0:span, :] = buf_ref[base:base + span, cols]
                for k in taps:
                    acc = acc + w_ref[k][:, cols] * sh_ref[k - r:k - r + CONV_SUB, :]
            acc_ref[rc * CONV_SUB:(rc + 1) * CONV_SUB, cols] = acc
    y = _layer_norm(acc_ref[...], g_ref[...], b_ref[...])
    o_ref[0] = (y * jax.nn.sigmoid(y)).astype(BF16)


def _conv_module(a, conv_w, conv_b, cln_g, cln_b):
    b, s, _ = a.shape
    per = CONV_ROWS // CONV_HALO
    return pl.pallas_call(
        _conv_kernel,
        out_shape=jax.ShapeDtypeStruct((b, s, CONV_DIM), BF16),
        grid=(b, s // CONV_ROWS),
        in_specs=[
            pl.BlockSpec((1, CONV_ROWS, CONV_DIM), lambda i, j: (i, j, 0)),
            pl.BlockSpec((1, CONV_HALO, CONV_DIM), lambda i, j: (i, jnp.maximum(j * per - 1, 0), 0)),
            _const_spec((CONV_WIDTH, 1, CONV_DIM)),
            _const_spec((1, CONV_DIM)),
            _const_spec((1, CONV_DIM)),
            _const_spec((1, CONV_DIM)),
        ],
        out_specs=pl.BlockSpec((1, CONV_ROWS, CONV_DIM), lambda i, j: (i, j, 0)),
        scratch_shapes=[pltpu.VMEM((CONV_HALO + CONV_ROWS, CONV_DIM), F32),
                        pltpu.VMEM((CONV_ROWS, CONV_DIM), F32),
                        pltpu.VMEM((CONV_SUB + (-(-CONV_WIDTH // SUBLANES) - 1) * SUBLANES, LANES), F32)],
        compiler_params=_cparams("parallel", "arbitrary"),
        name="conv_module",
    )(a, a, conv_w.reshape(CONV_WIDTH, 1, CONV_DIM), conv_b.reshape(1, -1),
      cln_g.reshape(1, -1), cln_b.reshape(1, -1))


def _compress_one(x_ref, pe_ref, w1_ref, w2_ref):
    nch = x_ref.shape[1]
    first = None
    second = None
    for l in range(CMP_STRIDE):
        xl = x_ref[0, :, l, :]
        pa = jnp.dot((xl + pe_ref[l]).astype(BF16), w1_ref[l], preferred_element_type=F32)
        pb = jnp.dot((xl + pe_ref[CMP_STRIDE + l]).astype(BF16), w1_ref[CMP_STRIDE + l],
                     preferred_element_type=F32)
        first = pa if first is None else first + pa
        second = pb if second is None else second + pb
    u = first + pltpu.roll(second, nch - 1, 0)
    hid = (u * jax.nn.sigmoid(u)).astype(BF16)
    out = jnp.dot(hid, w2_ref[...], preferred_element_type=F32)
    row = lax.broadcasted_iota(jnp.int32, out.shape, 0)
    return jnp.where(row < nch - 1, out, 0.0)


def _compress_kernel(xk_ref, xv_ref, pek_ref, w1k_ref, w2k_ref, pev_ref, w1v_ref, w2v_ref, feat_ref,
                     kc_ref, vct_ref):
    kc = _compress_one(xk_ref, pek_ref, w1k_ref, w2k_ref)
    vct = _compress_one(xv_ref, pev_ref, w1v_ref, w2v_ref).T
    lane = lax.broadcasted_iota(jnp.int32, kc.shape, 1)
    for g in range(KV_GROUPS):
        keys = kc if g == 0 else pltpu.roll(kc, GD - g * HEAD_DIM, 1)
        kc_ref[0, g] = jnp.where(lane < HEAD_DIM, keys, feat_ref[...]).astype(BF16)
        vct_ref[0, g] = vct[g * HEAD_DIM:(g + 1) * HEAD_DIM, :].astype(BF16)


def _compress_weights(pe, w1, w2):
    eye = jnp.eye(KV_GROUPS, dtype=F32)
    pe2 = jnp.tile(pe, (1, KV_GROUPS)).reshape(CMP_LEN, 1, GD)
    w1r = w1.reshape(CMP_LEN, HEAD_DIM, CMP_HIDDEN)
    w1bd = jnp.einsum('ldh,gk->lgdkh', w1r, eye).reshape(CMP_LEN, GD, KV_GROUPS * CMP_HIDDEN)
    w2bd = jnp.einsum('hd,gk->ghkd', w2, eye).reshape(KV_GROUPS * CMP_HIDDEN, GD)
    return pe2, w1bd.astype(BF16), w2bd.astype(BF16)


def _compress(kcr, vcr, pe_k, w1_k, w2_k, pe_v, w1_v, w2_v):
    b, s, _ = kcr.shape
    nch = s // CMP_STRIDE
    xk = kcr.reshape(b, nch, CMP_STRIDE, GD)
    xv = vcr.reshape(b, nch, CMP_STRIDE, GD)
    wk = _compress_weights(pe_k, w1_k, w2_k)
    wv = _compress_weights(pe_v, w1_v, w2_v)
    xspec = pl.BlockSpec((1, nch, CMP_STRIDE, GD), lambda i: (i, 0, 0, 0))
    wspecs = [_const_spec((CMP_LEN, 1, GD)), _const_spec((CMP_LEN, GD, KV_GROUPS * CMP_HIDDEN)),
              _const_spec((KV_GROUPS * CMP_HIDDEN, GD))]
    assert GD == WIN_K
    feat = jnp.asarray(_position_features(np.arange(nch) * CMP_STRIDE + CMP_LEN - 1))
    return pl.pallas_call(
        _compress_kernel,
        out_shape=[jax.ShapeDtypeStruct((b, KV_GROUPS, nch, WIN_K), BF16),
                   jax.ShapeDtypeStruct((b, KV_GROUPS, HEAD_DIM, nch), BF16)],
        grid=(b,),
        in_specs=[xspec, xspec] + wspecs + wspecs + [_const_spec((nch, WIN_K))],
        out_specs=[pl.BlockSpec((1, KV_GROUPS, nch, WIN_K), lambda i: (i, 0, 0, 0)),
                   pl.BlockSpec((1, KV_GROUPS, HEAD_DIM, nch), lambda i: (i, 0, 0, 0))],
        compiler_params=_cparams("parallel"),
        name="compress",
    )(xk, xv, *wk, *wv, feat)


def _tile_scores(tiles, q_rhs):
    scores = []
    for k, _, mask in tiles:
        s = jnp.dot(k, q_rhs, preferred_element_type=F32)
        scores.append(s if mask is None else s + mask)
    return scores


def _softmax_update(carry, scores, tiles):
    m, l, acc = carry
    m_new = m
    for s in scores:
        m_new = jnp.maximum(m_new, jnp.max(s, axis=0, keepdims=True))
    probs = [jnp.exp2(s - m_new) for s in scores]
    alpha = jnp.exp2(m - m_new)
    l = alpha * l
    for p in probs:
        l = l + jnp.sum(p, axis=0, keepdims=True)
    p_all = jnp.concatenate([p.astype(BF16) for p in probs], axis=0)
    v_all = jnp.concatenate([v for _, v, _ in tiles], axis=1)
    acc = alpha * acc + jnp.dot(v_all, p_all, preferred_element_type=F32)
    return m_new, l, acc


def _attend_step(carry, tiles, q_rhs):
    return _softmax_update(carry, _tile_scores(tiles, q_rhs), tiles)


def _nsa_kernel(qt_ref, kc_ref, vct_ref, ksel_ref, kwin_ref, vt_ref, gt_ref, ovt_ref, tab_ref,
                knull_ref, slope_ref, o_ref, qrhs_ref, words_ref, list_ref):
    qb = pl.program_id(2)
    t0 = qb * Q_BLOCK
    nsel = ovt_ref.shape[0]
    qt = qt_ref[0, 0]
    qcat = jnp.concatenate([qt[j * HEAD_DIM:(j + 1) * HEAD_DIM, :] for j in range(HPG)], axis=1)
    slope = slope_ref[0]

    qpos_row = t0 + (lax.broadcasted_iota(jnp.int32, (1, QROWS), 1) & (Q_BLOCK - 1))
    q_hi = (qpos_row >> SEL_SHIFT).astype(F32)
    q_lo = (qpos_row & (SEL_BLOCK - 1)).astype(F32)
    coef = slope * LOG2E

    def split(x):
        rows = []
        for _ in range(FEAT_SPLIT):
            part = x.astype(BF16).astype(F32)
            rows.append(part)
            x = x - part
        return rows

    feat = jnp.concatenate(split(coef * SEL_BLOCK) + split(coef) + split(-(coef * SEL_BLOCK) * q_hi)
                           + split(-coef * q_lo) + [jnp.full((1, QROWS), MASK_NEG, F32)]
                           + [jnp.zeros((FEAT_COLS - 4 * FEAT_SPLIT - 1, QROWS), F32)], axis=0)
    q_win = jnp.concatenate([qcat.astype(F32), feat,
                             jnp.zeros((WIN_K - HEAD_DIM - FEAT_COLS, QROWS), F32)], axis=0).astype(BF16)
    qrhs_ref[0:WIN_K, :] = q_win

    init = (jnp.full((1, QROWS), NEG_BIG, F32), jnp.zeros((1, QROWS), F32),
            jnp.zeros((HEAD_DIM, QROWS), F32))

    def key_tile(k_ref, t_raw, v_lo, mask=None, maybe_null=True):
        k0 = pl.multiple_of(jnp.maximum(t_raw, 0) * KEY_TILE, KEY_TILE)
        keys = k_ref[0, 0, pl.ds(k0, KEY_TILE), :]
        if maybe_null:
            keys = jnp.where(t_raw < 0, knull_ref[:, 0:keys.shape[1]], keys)
        return keys, vt_ref[0, 0, v_lo:v_lo + HEAD_DIM, pl.ds(k0, KEY_TILE)], mask

    nc = kc_ref.shape[2]
    s = jnp.dot(kc_ref[0, 0], q_win, preferred_element_type=F32)
    win_tiles = []
    for w in range(WIN_TILES):
        edge = tab_ref[0] if w == 0 else (tab_ref[1] if w == WIN_TILES - 1 else None)
        win_tiles.append(key_tile(kwin_ref, qb - (WIN_TILES - 1) + w, HEAD_DIM, edge, w < WIN_TILES - 1))
    win_scores = _tile_scores(win_tiles, q_win)
    cend = lax.broadcasted_iota(jnp.int32, (nc, QROWS), 0) * CMP_STRIDE + (CMP_LEN - 1)
    qpos = t0 + (lax.broadcasted_iota(jnp.int32, (nc, QROWS), 1) & (Q_BLOCK - 1))
    s = jnp.where(qpos >= cend, s, -jnp.inf)
    m_c = jnp.maximum(jnp.max(s, axis=0, keepdims=True), NEG_BIG)
    p = jnp.exp2(s - m_c)
    l_c = jnp.sum(p, axis=0, keepdims=True)
    pb = p.astype(BF16)
    inv_c = jnp.where(l_c > 0, 1.0 / l_c, 0.0)
    o_cmp = jnp.dot(vct_ref[0, 0], pb, preferred_element_type=F32) * inv_c
    imp4 = jnp.dot(ovt_ref[...], pb, preferred_element_type=F32) * inv_c
    imp = imp4[:, 0:Q_BLOCK]
    for j in range(1, HPG):
        imp = imp + imp4[:, j * Q_BLOCK:(j + 1) * Q_BLOCK]

    _, l_w, acc_w = _softmax_update(init, win_scores, win_tiles)

    n_iota = lax.broadcasted_iota(jnp.int32, (nsel, Q_BLOCK), 0)
    cur = (t0 + lax.broadcasted_iota(jnp.int32, (nsel, Q_BLOCK), 1)) >> SEL_SHIFT
    valid = n_iota <= cur
    forced = (n_iota == 0) | (n_iota == cur) | (n_iota == cur - 1)
    score = jnp.where(valid, jnp.where(forced, FORCE_SCORE, imp), -1.0)
    selm = jnp.full((nsel, Q_BLOCK), MASK_NEG, F32)

    def pick(carry):
        score, selm = carry
        best = jnp.max(score, axis=0, keepdims=True)
        first = jnp.min(jnp.where(score == best, n_iota, nsel), axis=0, keepdims=True)
        hit = n_iota == first
        selm = jnp.where(hit & (best >= 0), 0.0, selm)
        return jnp.where(hit, -2.0, score), selm

    carry = (score, selm)
    for _ in range(min(SEL_TOPN, nsel)):
        carry = pick(carry)
    selm = carry[1]
    qrhs_ref[WIN_K:WIN_K + nsel, :] = jnp.concatenate([selm] * HPG, axis=1).astype(BF16)
    if nsel < SEL_K - WIN_K:
        qrhs_ref[WIN_K + nsel:SEL_K, :] = jnp.zeros((SEL_K - WIN_K - nsel, QROWS), BF16)

    active = jnp.max(jnp.where(selm == 0.0, 1.0, 0.0), axis=1, keepdims=True)
    weight = jnp.left_shift(1, lax.broadcasted_iota(jnp.int32, (nsel, 1), 0) & (WORD_BLOCKS - 1))
    packed = active * weight.astype(F32)
    for w in range(nsel // WORD_BLOCKS):
        words_ref[w] = jnp.sum(packed[w * WORD_BLOCKS:(w + 1) * WORD_BLOCKS, :]).astype(jnp.int32)

    tiles_per_word = WORD_BLOCKS // TILE_BLOCKS

    def compact(w, n):
        word = words_ref[w]
        for u in range(tiles_per_word):
            t = w * tiles_per_word + u
            hit = (((word >> (u * TILE_BLOCKS)) & ((1 << TILE_BLOCKS) - 1)) != 0) & (t < qb)
            list_ref[n] = t
            n = n + hit.astype(jnp.int32)
        return n

    n_act = lax.fori_loop(0, (qb >> (tiles_per_word.bit_length() - 1)) + 1, compact, 0)

    def sel_step(i, carry):
        tiles = [key_tile(ksel_ref, list_ref[i * SEL_UNROLL + u], 0, None, False) for u in range(SEL_UNROLL)]
        return _attend_step(carry, tiles, qrhs_ref[...])

    n_full = n_act // SEL_UNROLL
    carry = lax.fori_loop(0, n_full, sel_step, init)
    last = jnp.maximum(n_act - 1, 0)
    tiles = []
    for u in range(SEL_UNROLL - 1):
        idx = n_full * SEL_UNROLL + u
        tiles.append(key_tile(ksel_ref, jnp.where(idx < n_act, list_ref[jnp.minimum(idx, last)], -1), 0))
    tiles.append(key_tile(ksel_ref, qb, 0, tab_ref[1], False))
    _, l_s, acc_s = _attend_step(carry, tiles, qrhs_ref[...])

    gt = gt_ref[0, 0]

    def gate_row(branch):
        return jnp.concatenate([gt[j * 3 + branch:j * 3 + branch + 1, :] for j in range(HPG)], axis=1)

    o_t = gate_row(0) * o_cmp + gate_row(1) * (acc_s / l_s) + gate_row(2) * (acc_w / l_w)
    out = jnp.concatenate([o_t[:, j * Q_BLOCK:(j + 1) * Q_BLOCK].T for j in range(HPG)], axis=1)
    o_ref[0] = out.astype(BF16)


def _nsa_tables(s):
    heads = np.arange(1, N_HEADS + 1, dtype=np.float32)
    slopes = (2.0 ** (-8.0 * heads / N_HEADS)).astype(np.float32).reshape(KV_GROUPS, HPG)
    slope_rows = np.repeat(slopes, Q_BLOCK, axis=1).reshape(KV_GROUPS, 1, QROWS)
    i = np.arange(KEY_TILE)[:, None]
    ql = np.tile(np.arange(Q_BLOCK), HPG)[None, :]
    neg = np.float32(-np.inf)
    tabs = np.stack([np.where(i > ql, np.float32(0), neg), np.where(i <= ql, np.float32(0), neg)], axis=0)
    knull = np.zeros((KEY_TILE, SEL_K), np.float32)
    knull[:, NULL_COL] = 1.0
    nch = s // CMP_STRIDE
    nsel = s // SEL_BLOCK
    c = np.arange(nch)
    n = np.arange(nsel)
    ov = ((c[None, :] * CMP_STRIDE + CMP_LEN - 1 >= n[:, None] * SEL_BLOCK)
          & (c[None, :] * CMP_STRIDE < (n[:, None] + 1) * SEL_BLOCK) & (c[None, :] < nch - 1))
    return jnp.asarray(slope_rows), jnp.asarray(tabs), jnp.asarray(knull, BF16), jnp.asarray(ov, BF16)


def _nsa(qt, kc, vct, ksel, kwin, vt, gt):
    b, _, _, s = qt.shape
    nch = s // CMP_STRIDE
    nsel = s // SEL_BLOCK
    assert nsel <= SEL_K - WIN_K and nsel % WORD_BLOCKS == 0
    slope_rows, tabs, knull, ovt = _nsa_tables(s)
    hq = HPG * HEAD_DIM
    per_group = lambda *shape: pl.BlockSpec((1, 1) + shape, lambda i, g, q: (i, g, 0, 0))
    return pl.pallas_call(
        _nsa_kernel,
        out_shape=jax.ShapeDtypeStruct((b, s, N_HEADS * HEAD_DIM), BF16),
        grid=(b, KV_GROUPS, s // Q_BLOCK),
        in_specs=[
            pl.BlockSpec((1, 1, hq, Q_BLOCK), lambda i, g, q: (i, g, 0, q)),
            per_group(nch, WIN_K),
            per_group(HEAD_DIM, nch),
            per_group(s, SEL_K),
            per_group(s, WIN_K),
            per_group(2 * HEAD_DIM, s),
            pl.BlockSpec((1, 1, GATE_ROWS, Q_BLOCK), lambda i, g, q: (i, g, 0, q)),
            _const_spec((nsel, nch)),
            _const_spec((2, KEY_TILE, QROWS)),
            _const_spec((KEY_TILE, SEL_K)),
            pl.BlockSpec((1, 1, QROWS), lambda i, g, q: (g, 0, 0)),
        ],
        out_specs=pl.BlockSpec((1, Q_BLOCK, hq), lambda i, g, q: (i, q, g)),
        scratch_shapes=[pltpu.VMEM((SEL_K, QROWS), BF16),
                        pltpu.SMEM((nsel // WORD_BLOCKS,), jnp.int32),
                        pltpu.SMEM((s // KEY_TILE + WORD_BLOCKS // TILE_BLOCKS,), jnp.int32)],
        compiler_params=_cparams("parallel", "parallel", "arbitrary"),
        name="nsa_attention",
    )(qt, kc, vct, ksel, kwin, vt, gt, ovt, tabs, knull, slope_rows)


OUT_ROWS = 512


def _outproj_kernel(h_ref, a_ref, o_ref, wa_ref, wo_ref, g_ref, b_ref, out_ref):
    m = (jnp.dot(a_ref[...], wa_ref[...], preferred_element_type=F32)
         + jnp.dot(o_ref[...], wo_ref[...], preferred_element_type=F32))
    out_ref[...] = _layer_norm(DN_ALPHA * h_ref[...] + m, g_ref[...], b_ref[...])


def _outproj_ln(h, a, o, w_out, g, b):
    n = h.shape[0]
    wa = w_out[:CONV_DIM].astype(BF16)
    wo = w_out[CONV_DIM:].astype(BF16)
    row = lambda width: pl.BlockSpec((OUT_ROWS, width), lambda i: (i, 0))
    return pl.pallas_call(
        _outproj_kernel,
        out_shape=jax.ShapeDtypeStruct((n, D_MODEL), F32),
        grid=(n // OUT_ROWS,),
        in_specs=[row(D_MODEL), row(CONV_DIM), row(QCOLS),
                  _const_spec((CONV_DIM, D_MODEL)), _const_spec((QCOLS, D_MODEL)),
                  _const_spec((1, D_MODEL)), _const_spec((1, D_MODEL))],
        out_specs=row(D_MODEL),
        compiler_params=_cparams("parallel"),
        name="mixer_out_ln",
    )(h, a, o, wa, wo, g.reshape(1, -1), b.reshape(1, -1))


S5_ROWS = 256
S5_FOLD = S5_CHUNK * LANES
S5_HALF = S5_SUPER * S5_STATE
SCAN_COLS = 2048
S5_OUT_COLS = 256


def _s5_operators(a_re, a_im, log_dt, b_re, b_im, c_re, c_im):
    hi = lax.Precision.HIGHEST
    dt = jnp.exp(log_dt)[:, None]
    ar, ai = a_re, a_im
    mag = jnp.exp(ar * dt)
    lr, li = mag * jnp.cos(ai * dt), mag * jnp.sin(ai * dt)
    den = ar * ar + ai * ai
    zr = ((lr - 1.0) * ar + li * ai) / den
    zi = (li * ar - (lr - 1.0) * ai) / den
    bbr = zr[..., None] * b_re - zi[..., None] * b_im
    bbi = zr[..., None] * b_im + zi[..., None] * b_re
    tau = jnp.arange(S5_CHUNK + 1, dtype=F32)[:, None, None]
    pmag = jnp.exp(ar * dt * tau)
    pr, pi = pmag * jnp.cos(ai * dt * tau), pmag * jnp.sin(ai * dt * tau)
    clr = c_re[None] * pr[:, :, None, :] - c_im[None] * pi[:, :, None, :]
    cli = c_re[None] * pi[:, :, None, :] + c_im[None] * pr[:, :, None, :]
    kern = (jnp.einsum('tgxp,gpc->gtxc', clr[:S5_CHUNK], bbr, precision=hi)
            - jnp.einsum('tgxp,gpc->gtxc', cli[:S5_CHUNK], bbi, precision=hi))
    eye = jnp.eye(S5_SUPER, dtype=F32)
    k8 = kern.reshape(N_SUPER, S5_SUPER, S5_CHUNK, S5_GROUP, S5_GROUP)
    bd = (k8.transpose(0, 2, 1, 4, 3)[:, :, :, :, None, :] * eye[None, None, :, None, :, None])
    bd = bd.reshape(N_SUPER, S5_CHUNK, LANES, LANES)
    rev = S5_CHUNK - 1 - jnp.arange(S5_CHUNK)
    ppr = pr[rev][:, :, :, None] * bbr[None] - pi[rev][:, :, :, None] * bbi[None]
    ppi = pr[rev][:, :, :, None] * bbi[None] + pi[rev][:, :, :, None] * bbr[None]
    p8 = jnp.stack([ppr, ppi], axis=0).reshape(2, S5_CHUNK, N_SUPER, S5_SUPER, S5_STATE, S5_GROUP)
    p_op = p8.transpose(2, 1, 3, 5, 0, 4).reshape(N_SUPER, S5_FOLD, 2 * S5_STATE)
    q8 = jnp.stack([clr[1:], -cli[1:]], axis=0).reshape(2, S5_CHUNK, N_SUPER, S5_SUPER, S5_GROUP, S5_STATE)
    q_op = q8.transpose(2, 0, 3, 5, 1, 4).reshape(N_SUPER, 2 * S5_HALF, S5_CHUNK * S5_GROUP)
    dr = pr[S5_CHUNK].reshape(1, -1)
    di = pi[S5_CHUNK].reshape(1, -1)
    return bd.astype(BF16), p_op.astype(BF16), q_op.astype(BF16), dr, di


def _fold_rows(x_ref):
    chunks = x_ref.shape[0] // S5_CHUNK
    return [x_ref[pl.ds(l, chunks, stride=S5_CHUNK), :] for l in range(S5_CHUNK)]


EXPAND_ROWS = 256


def _expand_block_diag(dst_ref, src_ref, rep_ref, row_shift, col_shift):
    n_rows, n_cols = dst_ref.shape
    for r0 in range(0, n_rows, EXPAND_ROWS):
        full = jnp.dot(src_ref[0, r0:r0 + EXPAND_ROWS, :], rep_ref[...], preferred_element_type=F32)
        row_g = ((lax.broadcasted_iota(jnp.int32, full.shape, 0) + r0) >> row_shift) & (S5_SUPER - 1)
        col_g = (lax.broadcasted_iota(jnp.int32, full.shape, 1) >> col_shift) & (S5_SUPER - 1)
        dst_ref[r0:r0 + EXPAND_ROWS, :] = jnp.where(row_g == col_g, full, 0.0).astype(dst_ref.dtype)


def _s5_state_kernel(x_ref, p_ref, rep_ref, vr_ref, vi_ref, pfull_ref):
    @pl.when(pl.program_id(1) == 0)
    def _():
        _expand_block_diag(pfull_ref, p_ref, rep_ref, S5_GROUP.bit_length() - 1, S5_STATE.bit_length() - 1)

    x2 = jnp.concatenate(_fold_rows(x_ref), axis=1).astype(BF16)
    v = jnp.dot(x2, pfull_ref[...], preferred_element_type=F32)
    vr_ref[...] = v[:, :S5_HALF]
    vi_ref[...] = v[:, S5_HALF:]


def _s5_scan_kernel(vr_ref, vi_ref, dr_ref, di_ref, hr_ref, hi_ref):
    dr = dr_ref[...]
    di = di_ref[...]

    def body(k, carry):
        hr, hi = carry
        hr_ref[0, pl.ds(k, 1), :] = hr
        hi_ref[0, pl.ds(k, 1), :] = hi
        vr = vr_ref[0, pl.ds(k, 1), :]
        vi = vi_ref[0, pl.ds(k, 1), :]
        return dr * hr - di * hi + vr, dr * hi + di * hr + vi

    zero = jnp.zeros((1, SCAN_COLS), F32)
    lax.fori_loop(0, vr_ref.shape[1], body, (zero, zero))


def _s5_out_kernel(x_ref, hr_ref, hi_ref, bd_ref, q_ref, rep_ref, d_ref, y_ref, m_ref, qfull_ref):
    @pl.when(pl.program_id(1) == 0)
    def _():
        _expand_block_diag(qfull_ref, q_ref, rep_ref, S5_STATE.bit_length() - 1, S5_GROUP.bit_length() - 1)
        zero = jnp.zeros((LANES, LANES), m_ref.dtype)
        for l_in in range(S5_CHUNK):
            for l_out in range(S5_CHUNK):
                blk = bd_ref[0, l_out - l_in] if l_out >= l_in else zero
                m_ref[l_in * LANES:(l_in + 1) * LANES, l_out * LANES:(l_out + 1) * LANES] = blk

    xs = _fold_rows(x_ref)
    x2 = jnp.concatenate(xs, axis=1).astype(BF16)
    hcat = jnp.concatenate([hr_ref[...], hi_ref[...]], axis=1).astype(BF16)
    d = d_ref[...]
    chunks = y_ref.shape[0] // S5_CHUNK
    per = S5_OUT_COLS // LANES
    for j in range(S5_FOLD // S5_OUT_COLS):
        cols = slice(j * S5_OUT_COLS, (j + 1) * S5_OUT_COLS)
        kdim = (j + 1) * S5_OUT_COLS
        y = (jnp.dot(x2[:, :kdim], m_ref[:kdim, cols], preferred_element_type=F32)
             + jnp.dot(hcat, qfull_ref[:, cols], preferred_element_type=F32))
        for i in range(per):
            l = j * per + i
            y_ref[pl.ds(l, chunks, stride=S5_CHUNK), :] = jax.nn.gelu(y[:, i * LANES:(i + 1) * LANES] + d * xs[l])


def _glu_ln_kernel(h_ref, y_ref, w_ref, g_ref, b_ref, o_ref):
    r = jnp.dot(y_ref[...].astype(BF16), w_ref[...], preferred_element_type=F32)
    m = r[:, :D_MODEL] * jax.nn.sigmoid(r[:, D_MODEL:])
    o_ref[...] = _layer_norm(DN_ALPHA * h_ref[...] + m, g_ref[...], b_ref[...])


def _s5_mixer_ln(h, nbatch, params, d_skip, w_glu, g, b):
    n = h.shape[0]
    nrow = n // S5_CHUNK
    rows = min(S5_ROWS, nrow)
    bd_op, p_op, q_op, dr, di = _s5_operators(*params)
    xspec = pl.BlockSpec((rows * S5_CHUNK, LANES), lambda s, i: (i, s))
    half = pl.BlockSpec((rows, S5_HALF), lambda s, i: (i, s))
    states = N_SUPER * S5_HALF
    rep_p = np.zeros((2, S5_STATE, 2, S5_SUPER, S5_STATE), np.float32)
    rep_p[np.arange(2)[:, None], np.arange(S5_STATE)[None, :], np.arange(2)[:, None], :, np.arange(S5_STATE)[None, :]] = 1.0
    rep_p = jnp.asarray(rep_p.reshape(2 * S5_STATE, 2 * S5_HALF), BF16)
    rep_q = np.zeros((S5_CHUNK, S5_GROUP, S5_CHUNK, S5_SUPER, S5_GROUP), np.float32)
    rep_q[np.arange(S5_CHUNK)[:, None], np.arange(S5_GROUP)[None, :], np.arange(S5_CHUNK)[:, None], :, np.arange(S5_GROUP)[None, :]] = 1.0
    rep_q = jnp.asarray(rep_q.reshape(S5_CHUNK * S5_GROUP, S5_FOLD), BF16)
    vr, vi = pl.pallas_call(
        _s5_state_kernel,
        out_shape=[jax.ShapeDtypeStruct((nrow, states), F32)] * 2,
        grid=(N_SUPER, nrow // rows),
        in_specs=[xspec, pl.BlockSpec((1, S5_FOLD, 2 * S5_STATE), lambda s, i: (s, 0, 0)),
                  _const_spec((2 * S5_STATE, 2 * S5_HALF))],
        out_specs=[half, half],
        scratch_shapes=[pltpu.VMEM((S5_FOLD, 2 * S5_HALF), BF16)],
        compiler_params=_cparams("arbitrary", "arbitrary"),
        name="s5_state_in",
    )(h, p_op, rep_p)
    nch = nrow // nbatch
    seq = pl.BlockSpec((1, nch, SCAN_COLS), lambda i, c: (i, 0, c))
    dspec = pl.BlockSpec((1, SCAN_COLS), lambda i, c: (0, c))
    hr, hi = pl.pallas_call(
        _s5_scan_kernel,
        out_shape=[jax.ShapeDtypeStruct((nbatch, nch, states), F32)] * 2,
        grid=(nbatch, states // SCAN_COLS),
        in_specs=[seq, seq, dspec, dspec],
        out_specs=[seq, seq],
        compiler_params=_cparams("parallel", "parallel"),
        name="s5_scan",
    )(vr.reshape(nbatch, nch, states), vi.reshape(nbatch, nch, states), dr, di)
    y = pl.pallas_call(
        _s5_out_kernel,
        out_shape=jax.ShapeDtypeStruct((n, D_MODEL), F32),
        grid=(N_SUPER, nrow // rows),
        in_specs=[xspec, half, half,
                  pl.BlockSpec((1, S5_CHUNK, LANES, LANES), lambda s, i: (s, 0, 0, 0)),
                  pl.BlockSpec((1, 2 * S5_HALF, S5_CHUNK * S5_GROUP), lambda s, i: (s, 0, 0)),
                  _const_spec((S5_CHUNK * S5_GROUP, S5_FOLD)),
                  pl.BlockSpec((1, LANES), lambda s, i: (0, s))],
        out_specs=xspec,
        scratch_shapes=[pltpu.VMEM((S5_FOLD, S5_FOLD), BF16), pltpu.VMEM((2 * S5_HALF, S5_FOLD), BF16)],
        compiler_params=_cparams("arbitrary", "arbitrary"),
        name="s5_out",
    )(h, hr.reshape(nrow, states), hi.reshape(nrow, states), bd_op, q_op, rep_q, d_skip.reshape(1, -1))
    row = lambda width: pl.BlockSpec((OUT_ROWS, width), lambda i: (i, 0))
    return pl.pallas_call(
        _glu_ln_kernel,
        out_shape=jax.ShapeDtypeStruct((n, D_MODEL), F32),
        grid=(n // OUT_ROWS,),
        in_specs=[row(D_MODEL), row(D_MODEL), _const_spec((D_MODEL, 2 * D_MODEL)),
                  _const_spec((1, D_MODEL)), _const_spec((1, D_MODEL))],
        out_specs=row(D_MODEL),
        compiler_params=_cparams("parallel"),
        name="s5_glu_ln",
    )(h, y, w_glu.astype(BF16), g.reshape(1, -1), b.reshape(1, -1))


def kernel(x, ffn1_w_in, ffn1_w_out, ffn2_w_in, ffn2_w_out, ln_g, ln_b, ev_w_in, ev_conv_w, ev_conv_b, ev_cln_g, ev_cln_b, ev_pe_k, ev_w1_k, ev_w2_k, ev_pe_v, ev_w1_v, ev_w2_v, ev_w_out, od_a_re, od_a_im, od_log_dt, od_b_re, od_b_im, od_c_re, od_c_im, od_d, od_w_glu):
    bsz, seq, dm = x.shape
    assert dm == D_MODEL and seq % PROJ_ROWS == 0 and seq >= 2 * WINDOW
    n = bsz * seq
    h = x.reshape(n, dm)
    for layer in range(DEPTH):
        i = layer // 2
        h = _ffn_ln(h, ffn1_w_in[layer], ffn1_w_out[layer], ln_g[layer, 0], ln_b[layer, 0])
        if layer % 2 == 0:
            a, kcr, vcr, ksel, kwin, qt, vt, gt = _even_proj(h.reshape(bsz, seq, dm), ev_w_in[i])
            a = _conv_module(a, ev_conv_w[i], ev_conv_b[i], ev_cln_g[i], ev_cln_b[i])
            kc, vct = _compress(kcr, vcr, ev_pe_k[i], ev_w1_k[i], ev_w2_k[i],
                                ev_pe_v[i], ev_w1_v[i], ev_w2_v[i])
            o = _nsa(qt, kc, vct, ksel, kwin, vt, gt)
            h = _outproj_ln(h, a.reshape(n, CONV_DIM), o.reshape(n, QCOLS), ev_w_out[i],
                            ln_g[layer, 1], ln_b[layer, 1])
        else:
            params = (od_a_re[i], od_a_im[i], od_log_dt[i], od_b_re[i], od_b_im[i],
                      od_c_re[i], od_c_im[i])
            h = _s5_mixer_ln(h, bsz, params, od_d[i], od_w_glu[i], ln_g[layer, 1], ln_b[layer, 1])
        h = _ffn_ln(h, ffn2_w_in[layer], ffn2_w_out[layer], ln_g[layer, 2], ln_b[layer, 2])
    return h.reshape(bsz, seq, dm)
```
